```python
import jax, jax.numpy as jnp
from jax import lax
import numpy as np

D_MODEL = 1024
BATCH = 32
SEQ = 2048
DEPTH = 1

D_MIX = D_MODEL
HEAD_DIM = 64
ATTN_WIDTH = D_MIX // 2
N_Q_HEADS = ATTN_WIDTH // HEAD_DIM
N_KV_HEADS = N_Q_HEADS // 4
KV_WIDTH = N_KV_HEADS * HEAD_DIM
CONV_WIDTH = D_MIX - ATTN_WIDTH
CONV_GROUPS = CONV_WIDTH // HEAD_DIM
CONV_KERNEL = 31
WINDOW = 128
BLOCK = 128
ROPE_THETA = 500000.0
ROT_DIM = HEAD_DIM // 4
Q_END = ATTN_WIDTH
K_END = Q_END + KV_WIDTH
V_END = K_END + KV_WIDTH
CA_END = V_END + CONV_WIDTH
IN_COLS = CA_END + CONV_WIDTH
N_GROUPS = 4
EXPERTS_PER_GROUP = 8
N_EXPERTS = N_GROUPS * EXPERTS_PER_GROUP
TOP_K = 2
D_EXPERT = D_MODEL // 2
EXPERT_BLOCK = 512
EPS = 1e-5

kernel_name = 'hymba_conformer_swa_sink_hmoe_block'


def rms_norm(x, w):
    xf = x.astype(jnp.float32)
    y = xf * lax.rsqrt(jnp.mean(xf * xf, axis=-1, keepdims=True) + EPS)
    return (y * w.astype(jnp.float32)).astype(x.dtype)


def layer_norm(x, w, b):
    xf = x.astype(jnp.float32)
    mu = jnp.mean(xf, axis=-1, keepdims=True)
    xc = xf - mu
    y = xc * lax.rsqrt(jnp.mean(xc * xc, axis=-1, keepdims=True) + EPS)
    return (y * w.astype(jnp.float32) + b.astype(jnp.float32)).astype(x.dtype)


def partial_rope(t, positions):
    half = ROT_DIM // 2
    inv_freq = jnp.power(ROPE_THETA, -jnp.arange(half, dtype=jnp.float32) * 2.0 / ROT_DIM)
    ang = positions.astype(jnp.float32)[..., None] * inv_freq
    cos = jnp.cos(ang)[:, :, None, :]
    sin = jnp.sin(ang)[:, :, None, :]
    tf = t.astype(jnp.float32)
    x1 = tf[..., :half]
    x2 = tf[..., half:ROT_DIM]
    out = jnp.concatenate([x1 * cos - x2 * sin, x2 * cos + x1 * sin, tf[..., ROT_DIM:]], axis=-1)
    return out.astype(t.dtype)


def sliding_window_sink_attention(q, k, v, sinks):
    B, S = q.shape[0], q.shape[1]
    nb = S // BLOCK
    G = N_Q_HEADS // N_KV_HEADS
    qb = q.reshape(B, nb, BLOCK, N_KV_HEADS, G, HEAD_DIM)
    kb = k.reshape(B, nb, BLOCK, N_KV_HEADS, HEAD_DIM)
    vb = v.reshape(B, nb, BLOCK, N_KV_HEADS, HEAD_DIM)

    def with_prev(t):
        prev = jnp.pad(t[:, :-1], ((0, 0), (1, 0), (0, 0), (0, 0), (0, 0)))
        return jnp.concatenate([prev, t], axis=2)

    kk = with_prev(kb)
    vv = with_prev(vb)
    s = jnp.einsum('bnqhgd,bnkhd->bnhgqk', qb.astype(jnp.float32), kk.astype(jnp.float32))
    s = s * (HEAD_DIM ** -0.5)
    qi = jnp.arange(BLOCK)[:, None]
    kj = jnp.arange(2 * BLOCK)[None, :]
    dist = qi + BLOCK - kj
    band = (dist >= 0) & (dist < WINDOW)
    valid = jnp.where((jnp.arange(nb) == 0)[:, None, None], band & (kj >= BLOCK), band)
    s = jnp.where(valid[None, :, None, None], s, -jnp.inf)
    sink = jnp.broadcast_to(sinks.astype(jnp.float32).reshape(N_KV_HEADS, G)[None, None, :, :, None, None],
                            s.shape[:-1] + (1,))
    p = jax.nn.softmax(jnp.concatenate([s, sink], axis=-1), axis=-1)[..., :-1]
    o = jnp.einsum('bnhgqk,bnkhd->bnqhgd', p.astype(v.dtype), vv)
    return o.reshape(B, S, N_Q_HEADS * HEAD_DIM)


def conformer_conv(a, gate, dw_w, dw_b, ln_w, ln_b):
    u = a * jax.nn.sigmoid(gate)
    u = lax.conv_general_dilated(u, dw_w[:, None, :], window_strides=(1,),
                                 padding=[(CONV_KERNEL - 1, 0)],
                                 dimension_numbers=('NWC', 'WIO', 'NWC'),
                                 feature_group_count=CONV_WIDTH) + dw_b
    u = layer_norm(u, ln_w, ln_b)
    return jax.nn.silu(u)


def routed_swiglu_experts(xf, expert_idx, gates, w_gate, w_up, w_down):
    N, D = xf.shape
    n_slots = N * TOP_K
    nb = -(-n_slots // EXPERT_BLOCK) + N_EXPERTS
    flat_e = expert_idx.reshape(-1)
    order = jnp.argsort(flat_e)
    sorted_e = flat_e[order]
    tok = (order // TOP_K).astype(jnp.int32)
    counts = jnp.bincount(flat_e, length=N_EXPERTS)
    padded = ((counts + EXPERT_BLOCK - 1) // EXPERT_BLOCK) * EXPERT_BLOCK
    cum_padded = jnp.cumsum(padded)
    pstart = cum_padded - padded
    start = jnp.cumsum(counts) - counts
    rank = jnp.arange(n_slots) - start[sorted_e]
    dest = pstart[sorted_e] + rank
    row_tok = jnp.full((nb * EXPERT_BLOCK,), N, jnp.int32).at[dest].set(tok)
    row_gate = jnp.zeros((nb * EXPERT_BLOCK,), xf.dtype).at[dest].set(gates.reshape(-1)[order])
    block_expert = jnp.minimum(
        jnp.searchsorted(cum_padded, jnp.arange(nb) * EXPERT_BLOCK, side='right'), N_EXPERTS - 1)
    xpad = jnp.concatenate([xf, jnp.zeros((1, D), xf.dtype)], axis=0)

    def expert_block(args):
        rows, e = args
        xb = xpad[rows]
        hb = jax.nn.silu(xb @ w_gate[e]) * (xb @ w_up[e])
        return hb @ w_down[e]

    yb = lax.map(expert_block, (row_tok.reshape(nb, EXPERT_BLOCK), block_expert))
    y = jax.ops.segment_sum(yb.reshape(-1, D) * row_gate[:, None], row_tok, num_segments=N + 1)
    return y[:N]


def hierarchical_moe(h, rg_w, rg_b, re_w, re_b, w_gate, w_up, w_down):
    B, S, D = h.shape
    xf = h.reshape(B * S, D)
    g_logits = (xf @ rg_w).astype(jnp.float32) + rg_b.astype(jnp.float32)
    g_prob = jax.nn.softmax(g_logits, axis=-1)
    g_idx = jnp.argmax(g_logits, axis=-1)
    g_p = jnp.take_along_axis(g_prob, g_idx[:, None], axis=1)[:, 0]
    e_logits = ((xf @ re_w).astype(jnp.float32) + re_b.astype(jnp.float32))
    e_logits = e_logits.reshape(-1, N_GROUPS, EXPERTS_PER_GROUP)
    e_logits = jnp.take_along_axis(e_logits, g_idx[:, None, None], axis=1)[:, 0]
    e_prob = jax.nn.softmax(e_logits, axis=-1)
    top_p, top_i = lax.top_k(e_prob, TOP_K)
    gates = (g_p[:, None] * top_p / jnp.sum(top_p, axis=-1, keepdims=True)).astype(h.dtype)
    expert_idx = (g_idx[:, None] * EXPERTS_PER_GROUP + top_i).astype(jnp.int32)
    y = routed_swiglu_experts(xf, expert_idx, gates, w_gate, w_up, w_down)
    return y.reshape(B, S, D)


def setup_inputs(seed: int = 0) -> dict:
    key = jax.random.key(seed)
    ks = jax.random.split(key, 24)
    f32 = jnp.float32

    def nrm(k, shape, scale):
        return jax.random.normal(k, shape, f32) * scale

    def gain(k, shape):
        return 1.0 + 0.02 * jax.random.normal(k, shape, f32)

    L = DEPTH
    return {
        'x': nrm(ks[0], (BATCH, SEQ, D_MODEL), 1.0),
        'positions': jnp.broadcast_to(jnp.arange(SEQ, dtype=jnp.int32), (BATCH, SEQ)),
        'attn_norm_w': gain(ks[1], (L, D_MODEL)),
        'w_in': nrm(ks[2], (L, D_MODEL, IN_COLS), D_MODEL ** -0.5),
        'attn_sinks': nrm(ks[3], (L, N_Q_HEADS), 0.5),
        'conv_dw_w': nrm(ks[4], (L, CONV_KERNEL, CONV_WIDTH), CONV_KERNEL ** -0.5),
        'conv_dw_b': nrm(ks[5], (L, CONV_WIDTH), 0.02),
        'conv_ln_w': gain(ks[6], (L, CONV_WIDTH)),
        'conv_ln_b': nrm(ks[7], (L, CONV_WIDTH), 0.02),
        'attn_out_norm_w': gain(ks[8], (L, ATTN_WIDTH)),
        'conv_out_norm_w': gain(ks[9], (L, CONV_WIDTH)),
        'w_out': nrm(ks[10], (L, D_MIX, D_MODEL), D_MIX ** -0.5),
        'ffn_norm_w': gain(ks[11], (L, D_MODEL)),
        'router_group_w': nrm(ks[12], (L, D_MODEL, N_GROUPS), D_MODEL ** -0.5),
        'router_group_b': nrm(ks[13], (L, N_GROUPS), 0.01),
        'router_expert_w': nrm(ks[14], (L, D_MODEL, N_EXPERTS), D_MODEL ** -0.5),
        'router_expert_b': nrm(ks[15], (L, N_EXPERTS), 0.01),
        'w_gate': nrm(ks[16], (L, N_EXPERTS, D_MODEL, D_EXPERT), D_MODEL ** -0.5),
        'w_up': nrm(ks[17], (L, N_EXPERTS, D_MODEL, D_EXPERT), D_MODEL ** -0.5),
        'w_down': nrm(ks[18], (L, N_EXPERTS, D_EXPERT, D_MODEL), D_EXPERT ** -0.5),
        'final_norm_w': gain(ks[19], (D_MODEL,)),
    }


def reference(x, positions, attn_norm_w, w_in, attn_sinks, conv_dw_w, conv_dw_b, conv_ln_w,
              conv_ln_b, attn_out_norm_w, conv_out_norm_w, w_out, ffn_norm_w, router_group_w,
              router_group_b, router_expert_w, router_expert_b, w_gate, w_up, w_down,
              final_norm_w):
    B, S = x.shape[0], x.shape[1]
    for l in range(DEPTH):
        hn = rms_norm(x, attn_norm_w[l])
        proj = hn @ w_in[l]
        q = proj[..., :Q_END].reshape(B, S, N_Q_HEADS, HEAD_DIM)
        k = proj[..., Q_END:K_END].reshape(B, S, N_KV_HEADS, HEAD_DIM)
        v = proj[..., K_END:V_END].reshape(B, S, N_KV_HEADS, HEAD_DIM)
        conv_a = proj[..., V_END:CA_END]
        conv_g = proj[..., CA_END:]
        q = partial_rope(q, positions)
        k = partial_rope(k, positions)
        attn = sliding_window_sink_attention(q, k, v, attn_sinks[l])
        conv = conformer_conv(conv_a, conv_g, conv_dw_w[l], conv_dw_b[l],
                              conv_ln_w[l], conv_ln_b[l])
        mixed = jnp.concatenate([rms_norm(attn, attn_out_norm_w[l]),
                                 rms_norm(conv, conv_out_norm_w[l])], axis=-1)
        x = x + mixed @ w_out[l]
        hf = rms_norm(x, ffn_norm_w[l])
        x = x + hierarchical_moe(hf, router_group_w[l], router_group_b[l], router_expert_w[l],
                                 router_expert_b[l], w_gate[l], w_up[l], w_down[l])
    return rms_norm(x, final_norm_w)
```

```python
import functools

import jax
import jax.numpy as jnp
from jax import lax
from jax.experimental import pallas as pl
from jax.experimental.pallas import tpu as pltpu

HEAD_DIM = 64
N_Q_HEADS = 8
N_KV_HEADS = 2
ATTN_WIDTH = N_Q_HEADS * HEAD_DIM
KV_WIDTH = N_KV_HEADS * HEAD_DIM
CONV_WIDTH = 512
CONV_KERNEL = 31
WINDOW = 128
ROPE_THETA = 500000.0
ROT_DIM = HEAD_DIM // 4
N_GROUPS = 4
EXPERTS_PER_GROUP = 8
N_EXPERTS = N_GROUPS * EXPERTS_PER_GROUP
TOP_K = 2
EPS = 1e-5

LANES = 128
SUBLANES = 8
VMEM_LIMIT_BYTES = 56 * 1024 * 1024

SEQ_TILE = 512
CONV_HALO = 32
CONV_CHUNK = 64
EXPERT_BLOCK = 512
COMBINE_TILE = 256
NEG_BIG = -1e30

Q_END = ATTN_WIDTH
K_END = Q_END + KV_WIDTH
V_END = K_END + KV_WIDTH
CA_END = V_END + CONV_WIDTH
IN_COLS = CA_END + CONV_WIDTH


def _rms(x, w):
    return x * lax.rsqrt(jnp.mean(x * x, axis=-1, keepdims=True) + EPS) * w


def _bdot(a, b):
    return jnp.dot(a.astype(jnp.bfloat16), b.astype(jnp.bfloat16),
                   preferred_element_type=jnp.float32)


def _bdot_t(a, b):
    return lax.dot_general(a.astype(jnp.bfloat16), b.astype(jnp.bfloat16),
                           (((1,), (1,)), ((), ())), preferred_element_type=jnp.float32)


def _mix_kernel(sinks_ref,
                x_ref, cos_ref, sin_ref, anw_ref, win_ref, dww_ref, dwb_ref, lnw_ref, lnb_ref,
                aonw_ref, conw_ref, wout_ref, fnw_ref, wr_ref, br_ref, tri_ref,
                x1_ref, hf_ref, ri_ref, rg_ref, cnt_ref,
                kd_ref, vd_ref, ubuf_ref, attn_ref, run_ref):
    b = pl.program_id(0)
    s = pl.program_id(1)
    ts = x_ref.shape[1]
    nblk = ts // WINDOW
    lane = lax.broadcasted_iota(jnp.int32, (ts, LANES), 1)

    @pl.when(s == 0)
    def _():
        kd_ref[:, 0:WINDOW, :] = jnp.zeros((4, WINDOW, LANES), jnp.bfloat16)
        vd_ref[:, 0:WINDOW, :] = jnp.zeros((4, WINDOW, LANES), jnp.bfloat16)
        ubuf_ref[0:CONV_HALO, :] = jnp.zeros((CONV_HALO, CONV_WIDTH), jnp.float32)

    @pl.when((s == 0) & (b == 0))
    def _():
        run_ref[...] = jnp.zeros_like(run_ref)

    x = x_ref[0]
    hn = _rms(x, anw_ref[...])
    proj = _bdot(hn, win_ref[...])

    cosv = cos_ref[0]
    sinv = sin_ref[0]
    first_half = (lane % HEAD_DIM) < (ROT_DIM // 2)

    def rope(t):
        partner = jnp.where(first_half, pltpu.roll(t, LANES - ROT_DIM // 2, 1),
                            pltpu.roll(t, ROT_DIM // 2, 1))
        return t * cosv + partner * sinv

    k = rope(proj[:, Q_END:K_END])
    v = proj[:, K_END:V_END]
    low = lane < HEAD_DIM
    for src, dst in ((k, kd_ref), (v, vd_ref)):
        rolled = pltpu.roll(src, HEAD_DIM, 1)
        zero = jnp.zeros_like(src)
        dst[0, WINDOW:, :] = jnp.where(low, src, zero).astype(jnp.bfloat16)
        dst[1, WINDOW:, :] = jnp.where(low, zero, rolled).astype(jnp.bfloat16)
        dst[2, WINDOW:, :] = jnp.where(low, rolled, zero).astype(jnp.bfloat16)
        dst[3, WINDOW:, :] = jnp.where(low, zero, src).astype(jnp.bfloat16)

    qi = lax.broadcasted_iota(jnp.int32, (WINDOW, 2 * WINDOW), 0)
    kj = lax.broadcasted_iota(jnp.int32, (WINDOW, 2 * WINDOW), 1)
    dist = qi + WINDOW - kj
    band = (dist >= 0) & (dist < WINDOW)
    first_lo = jnp.where(s == 0, WINDOW, 0)
    lane_q = lax.broadcasted_iota(jnp.int32, (WINDOW, LANES), 1)
    for p in range(N_Q_HEADS // 2):
        g = p // 2
        qs = rope(proj[:, p * LANES:(p + 1) * LANES]) * (HEAD_DIM ** -0.5)
        for j in range(nblk):
            rows = slice(j * WINDOW, (j + 1) * WINDOW)
            keys = slice(j * WINDOW, (j + 2) * WINDOW)
            valid = (band & (kj >= first_lo)) if j == 0 else band
            qb = qs[rows]
            out = None
            inv = []
            for half in range(2):
                sink = sinks_ref[2 * p + half]
                sc = _bdot_t(qb, kd_ref[2 * g + half, keys, :])
                sc = jnp.where(valid, sc, NEG_BIG)
                m = jnp.maximum(jnp.max(sc, axis=-1, keepdims=True), sink)
                e = jnp.exp(sc - m)
                inv.append(1.0 / (jnp.sum(e, axis=-1, keepdims=True) + jnp.exp(sink - m)))
                o = _bdot(e, vd_ref[2 * g + half, keys, :])
                out = o if out is None else out + o
            attn_ref[rows, p * LANES:(p + 1) * LANES] = out * jnp.where(
                lane_q < HEAD_DIM, inv[0], inv[1])
    for c in range(4):
        kd_ref[c, 0:WINDOW, :] = kd_ref[c, ts:ts + WINDOW, :]
        vd_ref[c, 0:WINDOW, :] = vd_ref[c, ts:ts + WINDOW, :]

    u = proj[:, V_END:CA_END] * jax.nn.sigmoid(proj[:, CA_END:IN_COLS])
    ubuf_ref[CONV_HALO:, :] = u
    dwb = dwb_ref[...]
    off0 = CONV_HALO - (CONV_KERNEL - 1)
    convs = []
    for c in range(ts // CONV_CHUNK):
        acc = jnp.broadcast_to(dwb, (CONV_CHUNK, CONV_WIDTH))
        for t in range(CONV_KERNEL):
            r0 = c * CONV_CHUNK + off0 + t
            acc = acc + ubuf_ref[r0:r0 + CONV_CHUNK, :] * dww_ref[t:t + 1, :]
        convs.append(acc)
    conv = jnp.concatenate(convs, axis=0)
    ubuf_ref[0:CONV_HALO, :] = ubuf_ref[ts:ts + CONV_HALO, :]
    mu = jnp.mean(conv, axis=-1, keepdims=True)
    xc = conv - mu
    y = xc * lax.rsqrt(jnp.mean(xc * xc, axis=-1, keepdims=True) + EPS) * lnw_ref[...] + lnb_ref[...]
    conv = y * jax.nn.sigmoid(y)

    mixed_a = _rms(attn_ref[...], aonw_ref[...])
    mixed_c = _rms(conv, conw_ref[...])
    x1 = x + _bdot(mixed_a, wout_ref[0:ATTN_WIDTH, :]) + _bdot(mixed_c, wout_ref[ATTN_WIDTH:, :])
    x1_ref[0] = x1
    hf = _rms(x1, fnw_ref[...])
    hf_ref[0] = hf

    hf_hi = hf.astype(jnp.bfloat16)
    hf_lo = (hf - hf_hi.astype(jnp.float32)).astype(jnp.bfloat16)
    logits = (jnp.dot(hf_hi, wr_ref[0], preferred_element_type=jnp.float32)
              + jnp.dot(hf_lo, wr_ref[0], preferred_element_type=jnp.float32)
              + jnp.dot(hf_hi, wr_ref[1], preferred_element_type=jnp.float32)) + br_ref[...]

    lanef = lane.astype(jnp.float32)
    big = float(LANES)
    gmask = lane < N_GROUPS
    gmax = jnp.max(jnp.where(gmask, logits, -jnp.inf), axis=-1, keepdims=True)
    gidx = jnp.min(jnp.where(gmask & (logits == gmax), lanef, big), axis=-1, keepdims=True)
    gsum = jnp.sum(jnp.where(gmask, jnp.exp(logits - gmax), 0.0), axis=-1, keepdims=True)
    g_p = 1.0 / gsum
    elo = N_GROUPS + EXPERTS_PER_GROUP * gidx
    emask = (lanef >= elo) & (lanef < elo + EXPERTS_PER_GROUP)
    em1 = jnp.max(jnp.where(emask, logits, -jnp.inf), axis=-1, keepdims=True)
    i1 = jnp.min(jnp.where(emask & (logits == em1), lanef, big), axis=-1, keepdims=True)
    mask2 = emask & (lanef != i1)
    em2 = jnp.max(jnp.where(mask2, logits, -jnp.inf), axis=-1, keepdims=True)
    i2 = jnp.min(jnp.where(mask2 & (logits == em2), lanef, big), axis=-1, keepdims=True)
    esum = jnp.sum(jnp.where(emask, jnp.exp(logits - em1), 0.0), axis=-1, keepdims=True)
    p1 = 1.0 / esum
    p2 = jnp.exp(em2 - em1) / esum
    gate1 = g_p * p1 / (p1 + p2)
    gate2 = g_p * p2 / (p1 + p2)
    e1 = i1 - N_GROUPS
    e2 = i2 - N_GROUPS

    oh1 = lanef == e1
    oh2 = lanef == e2
    onehot = jnp.where(oh1 | oh2, 1.0, 0.0)
    tot = jnp.dot(tri_ref[...], onehot.astype(jnp.bfloat16),
                  preferred_element_type=jnp.float32) + run_ref[0:1, :]
    r1 = jnp.sum(jnp.where(oh1, tot, 0.0), axis=-1, keepdims=True)
    r2 = jnp.sum(jnp.where(oh2, tot, 0.0), axis=-1, keepdims=True)
    run_ref[...] = run_ref[...] + jnp.sum(onehot, axis=0, keepdims=True)
    cnt_ref[...] = run_ref[...]

    ri = jnp.where(lane == 0, e1, jnp.where(lane == 1, e2, jnp.where(lane == 2, r1, r2)))
    ri_ref[0] = ri.astype(jnp.int32)
    rg_ref[0] = jnp.where(lane == 0, gate1, gate2)


def _mix(x, cos_t, sin_t, p):
    bsz, seq, d = x.shape
    ts = min(SEQ_TILE, seq)
    grid = (bsz, seq // ts)
    tile = lambda last: pl.BlockSpec((1, ts, last), lambda b, s, *_: (b, s, 0))
    full = lambda a: pl.BlockSpec(a.shape, lambda b, s, *_: (0,) * a.ndim)
    weights = [p["anw"], p["win"], p["dww"], p["dwb"], p["lnw"], p["lnb"], p["aonw"], p["conw"],
               p["wout"], p["fnw"], p["wr"], p["br"], p["tri"]]
    out_shape = [
        jax.ShapeDtypeStruct((bsz, seq, d), jnp.float32),
        jax.ShapeDtypeStruct((bsz, seq, d), jnp.float32),
        jax.ShapeDtypeStruct((bsz, seq, LANES), jnp.int32),
        jax.ShapeDtypeStruct((bsz, seq, LANES), jnp.float32),
        jax.ShapeDtypeStruct((SUBLANES, LANES), jnp.float32),
    ]
    out_specs = [tile(d), tile(d), tile(LANES), tile(LANES),
                 pl.BlockSpec((SUBLANES, LANES), lambda b, s, *_: (0, 0))]
    grid_spec = pltpu.PrefetchScalarGridSpec(
        num_scalar_prefetch=1, grid=grid,
        in_specs=[tile(d), tile(LANES), tile(LANES)] + [full(w) for w in weights],
        out_specs=out_specs,
        scratch_shapes=[
            pltpu.VMEM((4, ts + WINDOW, LANES), jnp.bfloat16),
            pltpu.VMEM((4, ts + WINDOW, LANES), jnp.bfloat16),
            pltpu.VMEM((ts + CONV_HALO, CONV_WIDTH), jnp.float32),
            pltpu.VMEM((ts, ATTN_WIDTH), jnp.float32),
            pltpu.VMEM((SUBLANES, LANES), jnp.float32),
        ])
    return pl.pallas_call(
        _mix_kernel, grid_spec=grid_spec, out_shape=out_shape, name="mix",
        compiler_params=pltpu.CompilerParams(
            dimension_semantics=("arbitrary", "arbitrary"),
            vmem_limit_bytes=VMEM_LIMIT_BYTES),
    )(p["sinks"], x, cos_t, sin_t, *weights)


def _row_copy(src_hbm, idx, dst, r, sem):
    return pltpu.make_async_copy(src_hbm.at[pl.ds(idx, 1), :], dst.at[pl.ds(r, 1), :], sem)


def _expert_kernel(bexp_ref, nused_ref,
                   rt_ref, rtn_ref, hf_hbm, wg_ref, wu_ref, wd_ref,
                   y_ref, buf_ref, sem_ref):
    i = pl.program_id(0)
    nused = nused_ref[0]
    bm = y_ref.shape[0]
    slot = i % 2

    def gather(idx_ref, sl, start):
        for r in range(bm):
            cp = _row_copy(hf_hbm, idx_ref[0, 0, r], buf_ref.at[sl], r, sem_ref.at[sl])
            cp.start() if start else cp.wait()

    @pl.when(i == 0)
    def _():
        gather(rt_ref, 0, True)

    @pl.when(i + 1 < nused)
    def _():
        gather(rtn_ref, 1 - slot, True)

    @pl.when(i < nused)
    def _():
        gather(rt_ref, slot, False)
        xb = buf_ref[slot].astype(jnp.bfloat16)
        hg = jnp.dot(xb, wg_ref[0], preferred_element_type=jnp.float32)
        hu = jnp.dot(xb, wu_ref[0], preferred_element_type=jnp.float32)
        h = hg * jax.nn.sigmoid(hg) * hu
        y_ref[...] = jnp.dot(h.astype(jnp.bfloat16), wd_ref[0], preferred_element_type=jnp.float32)

    @pl.when(i >= nused)
    def _():
        y_ref[...] = jnp.zeros_like(y_ref)


def _experts(hf2d, row_tok, block_expert, nused, wg, wu, wd, bm):
    n, d = hf2d.shape
    nb = block_expert.shape[0]
    de = wg.shape[2]
    rt3 = row_tok.reshape(nb, 1, bm)
    last = lambda i, be, nu: jnp.minimum(i, nu[0] - 1)
    grid_spec = pltpu.PrefetchScalarGridSpec(
        num_scalar_prefetch=2, grid=(nb,),
        in_specs=[
            pl.BlockSpec((1, 1, bm), lambda i, be, nu: (last(i, be, nu), 0, 0),
                         memory_space=pltpu.SMEM),
            pl.BlockSpec((1, 1, bm), lambda i, be, nu: (jnp.minimum(i + 1, nu[0] - 1), 0, 0),
                         memory_space=pltpu.SMEM),
            pl.BlockSpec(memory_space=pl.ANY),
            pl.BlockSpec((1, d, de), lambda i, be, nu: (be[last(i, be, nu)], 0, 0)),
            pl.BlockSpec((1, d, de), lambda i, be, nu: (be[last(i, be, nu)], 0, 0)),
            pl.BlockSpec((1, de, d), lambda i, be, nu: (be[last(i, be, nu)], 0, 0)),
        ],
        out_specs=pl.BlockSpec((bm, d), lambda i, be, nu: (i, 0)),
        scratch_shapes=[pltpu.VMEM((2, bm, d), jnp.float32), pltpu.SemaphoreType.DMA((2,))])
    return pl.pallas_call(
        _expert_kernel, grid_spec=grid_spec, name="experts",
        out_shape=jax.ShapeDtypeStruct((nb * bm, d), jnp.float32),
        compiler_params=pltpu.CompilerParams(
            dimension_semantics=("arbitrary",), vmem_limit_bytes=VMEM_LIMIT_BYTES),
    )(block_expert, nused, rt3, rt3, hf2d, wg, wu, wd)


def _combine_kernel(dst_ref, dstn_ref, x1_ref, rg_ref, fw_ref, ys_hbm, o_ref, buf_ref, sem_ref):
    i = pl.program_id(0)
    n = pl.num_programs(0)
    tc = x1_ref.shape[0]
    slot = i % 2

    def gather(idx_ref, sl, start):
        for r in range(2 * tc):
            cp = _row_copy(ys_hbm, idx_ref[0, 0, r], buf_ref.at[sl], r, sem_ref.at[sl])
            cp.start() if start else cp.wait()

    @pl.when(i == 0)
    def _():
        gather(dst_ref, 0, True)

    @pl.when(i + 1 < n)
    def _():
        gather(dstn_ref, 1 - slot, True)

    gather(dst_ref, slot, False)
    g = rg_ref[...]
    y = buf_ref[slot, 0:tc, :] * g[:, 0:1] + buf_ref[slot, tc:2 * tc, :] * g[:, 1:2]
    o_ref[...] = _rms(x1_ref[...] + y, fw_ref[...])


def _combine(x1, rg, ys, dest, fw):
    n, d = x1.shape
    tc = min(COMBINE_TILE, n)
    nt = n // tc
    d3 = dest.reshape(nt, tc, TOP_K).transpose(0, 2, 1).reshape(nt, 1, TOP_K * tc)
    return pl.pallas_call(
        _combine_kernel, grid=(nt,), name="combine",
        in_specs=[
            pl.BlockSpec((1, 1, TOP_K * tc), lambda i: (i, 0, 0), memory_space=pltpu.SMEM),
            pl.BlockSpec((1, 1, TOP_K * tc), lambda i: (jnp.minimum(i + 1, nt - 1), 0, 0),
                         memory_space=pltpu.SMEM),
            pl.BlockSpec((tc, d), lambda i: (i, 0)),
            pl.BlockSpec((tc, LANES), lambda i: (i, 0)),
            pl.BlockSpec((1, d), lambda i: (0, 0)),
            pl.BlockSpec(memory_space=pl.ANY),
        ],
        out_specs=pl.BlockSpec((tc, d), lambda i: (i, 0)),
        out_shape=jax.ShapeDtypeStruct((n, d), jnp.float32),
        scratch_shapes=[pltpu.VMEM((2, TOP_K * tc, d), jnp.float32), pltpu.SemaphoreType.DMA((2,))],
        compiler_params=pltpu.CompilerParams(
            dimension_semantics=("arbitrary",), vmem_limit_bytes=VMEM_LIMIT_BYTES),
    )(d3, d3, x1, rg, fw, ys)


def _rope_tables(positions):
    half = ROT_DIM // 2
    inv_freq = jnp.power(ROPE_THETA, -jnp.arange(half, dtype=jnp.float32) * 2.0 / ROT_DIM)
    ang = positions.astype(jnp.float32)[..., None] * inv_freq
    j = jnp.arange(LANES) % HEAD_DIM
    rot = j < ROT_DIM
    cos_t = jnp.where(rot, jnp.take(jnp.cos(ang), j % half, axis=-1), 1.0)
    sin_t = jnp.where(rot, jnp.take(jnp.sin(ang), j % half, axis=-1), 0.0)
    sin_t = jnp.where(j < half, -sin_t, sin_t)
    return cos_t, sin_t


def _layer(x, cos_t, sin_t, l, attn_norm_w, w_in, attn_sinks, conv_dw_w, conv_dw_b, conv_ln_w,
           conv_ln_b, attn_out_norm_w, conv_out_norm_w, w_out, ffn_norm_w, router_group_w,
           router_group_b, router_expert_w, router_expert_b, w_gate, w_up, w_down, out_norm_w):
    bsz, seq, d = x.shape
    n = bsz * seq
    bf = jnp.bfloat16
    row = lambda a: a.reshape(1, -1)
    wr = jnp.zeros((d, LANES), jnp.float32)
    wr = wr.at[:, :N_GROUPS].set(router_group_w[l]).at[:, N_GROUPS:N_GROUPS + N_EXPERTS].set(
        router_expert_w[l])
    wr_hi = wr.astype(bf)
    wr_lo = (wr - wr_hi.astype(jnp.float32)).astype(bf)
    br = jnp.zeros((1, LANES), jnp.float32)
    br = br.at[0, :N_GROUPS].set(router_group_b[l]).at[0, N_GROUPS:N_GROUPS + N_EXPERTS].set(
        router_expert_b[l])
    ts = min(SEQ_TILE, seq)
    params = dict(
        sinks=attn_sinks[l], anw=row(attn_norm_w[l]), win=w_in[l].astype(bf), dww=conv_dw_w[l],
        dwb=row(conv_dw_b[l]), lnw=row(conv_ln_w[l]), lnb=row(conv_ln_b[l]),
        aonw=row(attn_out_norm_w[l]), conw=row(conv_out_norm_w[l]), wout=w_out[l].astype(bf),
        fnw=row(ffn_norm_w[l]), wr=jnp.stack([wr_hi, wr_lo]), br=br,
        tri=jnp.tri(ts, k=-1, dtype=bf))
    x1, hf, ri, rg, cnt = _mix(x, cos_t, sin_t, params)

    bm = EXPERT_BLOCK
    nb = (n * TOP_K) // bm + N_EXPERTS
    counts = cnt[0, :N_EXPERTS].astype(jnp.int32)
    padded = ((counts + bm - 1) // bm) * bm
    cum = jnp.cumsum(padded)
    pstart = cum - padded
    nused = (cum[-1] // bm).astype(jnp.int32).reshape(1)
    ri2 = ri.reshape(n, LANES)
    dest = pstart[ri2[:, 0:TOP_K]] + ri2[:, TOP_K:2 * TOP_K]
    tok = jnp.broadcast_to(jnp.arange(n, dtype=jnp.int32)[:, None], (n, TOP_K))
    row_tok = jnp.zeros((nb * bm,), jnp.int32).at[dest.reshape(-1)].set(tok.reshape(-1))
    block_expert = jnp.minimum(
        jnp.searchsorted(cum, jnp.arange(nb, dtype=jnp.int32) * bm, side="right"),
        N_EXPERTS - 1).astype(jnp.int32)

    ys = _experts(hf.reshape(n, d), row_tok, block_expert, nused,
                  w_gate[l].astype(bf), w_up[l].astype(bf), w_down[l].astype(bf), bm)
    out = _combine(x1.reshape(n, d), rg.reshape(n, LANES), ys, dest, row(out_norm_w))
    return out.reshape(bsz, seq, d)


def kernel(x, positions, attn_norm_w, w_in, attn_sinks, conv_dw_w, conv_dw_b, conv_ln_w, conv_ln_b, attn_out_norm_w, conv_out_norm_w, w_out, ffn_norm_w, router_group_w, router_group_b, router_expert_w, router_expert_b, w_gate, w_up, w_down, final_norm_w):
    depth = attn_norm_w.shape[0]
    assert depth == 1, "the combine kernel applies the final norm, so exactly one layer is fused"
    cos_t, sin_t = _rope_tables(positions)
    return _layer(x, cos_t, sin_t, 0, attn_norm_w, w_in, attn_sinks, conv_dw_w, conv_dw_b,
                  conv_ln_w, conv_ln_b, attn_out_norm_w, conv_out_norm_w, w_out, ffn_norm_w,
                  router_group_w, router_group_b, router_expert_w, router_expert_b,
                  w_gate, w_up, w_down, final_norm_w)
```

```python
import jax
import jax.numpy as jnp
from jax import lax
from jax.experimental import pallas as pl
from jax.experimental.pallas import tpu as pltpu

HEAD_DIM = 64
N_Q_HEADS = 8
N_KV_HEADS = 2
ATTN_WIDTH = N_Q_HEADS * HEAD_DIM
KV_WIDTH = N_KV_HEADS * HEAD_DIM
CONV_WIDTH = 512
CONV_KERNEL = 31
WINDOW = 128
ROPE_THETA = 500000.0
ROT_DIM = HEAD_DIM // 4
N_GROUPS = 4
EXPERTS_PER_GROUP = 8
N_EXPERTS = N_GROUPS * EXPERTS_PER_GROUP
TOP_K = 2
EPS = 1e-5

LANES = 128
SUBLANES = 8
VMEM_LIMIT_BYTES = 56 * 1024 * 1024

SEQ_TILE = 512
CONV_HALO = 32
CONV_CHUNK = 64
DISPATCH_TILE = 512
EXPERT_BLOCK = 512
COMBINE_TILE = 256
INDEX_BATCH = 16
NEG_BIG = -1e30

Q_END = ATTN_WIDTH
K_END = Q_END + KV_WIDTH
V_END = K_END + KV_WIDTH
CA_END = V_END + CONV_WIDTH
IN_COLS = CA_END + CONV_WIDTH


def _rms(x, w):
    return x * lax.rsqrt(jnp.mean(x * x, axis=-1, keepdims=True) + EPS) * w


def _bdot(a, b):
    return jnp.dot(a.astype(jnp.bfloat16), b.astype(jnp.bfloat16),
                   preferred_element_type=jnp.float32)


def _bdot_t(a, b):
    return lax.dot_general(a.astype(jnp.bfloat16), b.astype(jnp.bfloat16),
                           (((1,), (1,)), ((), ())), preferred_element_type=jnp.float32)


def _load_tiled(ref, base, rows):
    d = SUBLANES * LANES
    return jnp.concatenate(
        [ref[pl.ds(base + j, rows, stride=SUBLANES), :] for j in range(d // LANES)], axis=1)


def _store_tiled(ref, val):
    for j in range(val.shape[1] // LANES):
        ref[pl.ds(j, val.shape[0], stride=SUBLANES), :] = val[:, j * LANES:(j + 1) * LANES]


def _mix_kernel(sinks_ref,
                x_ref, cos_ref, sin_ref, anw_ref, win_ref, dww_ref, dwb_ref, lnw_ref, lnb_ref,
                aonw_ref, conw_ref, wout_ref, fnw_ref, wr_ref, br_ref, tri_ref,
                x1_ref, hf_ref, ri_ref, rg_ref, cnt_ref,
                kd_ref, vd_ref, ubuf_ref, ush_ref, attn_ref, run_ref):
    b = pl.program_id(0)
    s = pl.program_id(1)
    ts = x_ref.shape[1]
    nblk = ts // WINDOW
    lane = lax.broadcasted_iota(jnp.int32, (ts, LANES), 1)

    @pl.when(s == 0)
    def _():
        kd_ref[:, 0:WINDOW, :] = jnp.zeros((4, WINDOW, LANES), jnp.bfloat16)
        vd_ref[:, 0:WINDOW, :] = jnp.zeros((4, WINDOW, LANES), jnp.bfloat16)
        ubuf_ref[0:CONV_HALO, :] = jnp.zeros((CONV_HALO, CONV_WIDTH), jnp.float32)

    @pl.when((s == 0) & (b == 0))
    def _():
        run_ref[...] = jnp.zeros_like(run_ref)

    x = x_ref[0]
    hn = _rms(x, anw_ref[...])
    proj = _bdot(hn, win_ref[...])

    cosv = cos_ref[0]
    sinv = sin_ref[0]
    first_half = (lane % HEAD_DIM) < (ROT_DIM // 2)

    def rope(t):
        partner = jnp.where(first_half, pltpu.roll(t, LANES - ROT_DIM // 2, 1),
                            pltpu.roll(t, ROT_DIM // 2, 1))
        return t * cosv + partner * sinv

    k = rope(proj[:, Q_END:K_END])
    v = proj[:, K_END:V_END]
    low = lane < HEAD_DIM
    for src, dst in ((k, kd_ref), (v, vd_ref)):
        rolled = pltpu.roll(src, HEAD_DIM, 1)
        zero = jnp.zeros_like(src)
        dst[0, WINDOW:, :] = jnp.where(low, src, zero).astype(jnp.bfloat16)
        dst[1, WINDOW:, :] = jnp.where(low, zero, rolled).astype(jnp.bfloat16)
        dst[2, WINDOW:, :] = jnp.where(low, rolled, zero).astype(jnp.bfloat16)
        dst[3, WINDOW:, :] = jnp.where(low, zero, src).astype(jnp.bfloat16)

    qi = lax.broadcasted_iota(jnp.int32, (WINDOW, 2 * WINDOW), 0)
    kj = lax.broadcasted_iota(jnp.int32, (WINDOW, 2 * WINDOW), 1)
    dist = qi + WINDOW - kj
    band = (dist >= 0) & (dist < WINDOW)
    first_lo = jnp.where(s == 0, WINDOW, 0)
    lane_q = lax.broadcasted_iota(jnp.int32, (WINDOW, LANES), 1)
    for p in range(N_Q_HEADS // 2):
        g = p // 2
        qs = rope(proj[:, p * LANES:(p + 1) * LANES]) * (HEAD_DIM ** -0.5)
        for j in range(nblk):
            rows = slice(j * WINDOW, (j + 1) * WINDOW)
            keys = slice(j * WINDOW, (j + 2) * WINDOW)
            valid = (band & (kj >= first_lo)) if j == 0 else band
            qb = qs[rows]
            out = None
            inv = []
            for half in range(2):
                sink = sinks_ref[2 * p + half]
                sc = _bdot_t(qb, kd_ref[2 * g + half, keys, :])
                sc = jnp.where(valid, sc, NEG_BIG)
                m = jnp.maximum(jnp.max(sc, axis=-1, keepdims=True), sink)
                e = jnp.exp(sc - m)
                inv.append(1.0 / (jnp.sum(e, axis=-1, keepdims=True) + jnp.exp(sink - m)))
                o = _bdot(e, vd_ref[2 * g + half, keys, :])
                out = o if out is None else out + o
            attn_ref[rows, p * LANES:(p + 1) * LANES] = out * jnp.where(
                lane_q < HEAD_DIM, inv[0], inv[1])
    for c in range(4):
        kd_ref[c, 0:WINDOW, :] = kd_ref[c, ts:ts + WINDOW, :]
        vd_ref[c, 0:WINDOW, :] = vd_ref[c, ts:ts + WINDOW, :]

    u = proj[:, V_END:CA_END] * jax.nn.sigmoid(proj[:, CA_END:IN_COLS])
    ubuf_ref[CONV_HALO:, :] = u
    span = ush_ref.shape[1]
    for sh in range(1, SUBLANES):
        ush_ref[sh - 1] = ubuf_ref[sh:sh + span, :]
    dwb = dwb_ref[...]
    off0 = CONV_HALO - (CONV_KERNEL - 1)
    convs = []
    for c in range(ts // CONV_CHUNK):
        acc = jnp.broadcast_to(dwb, (CONV_CHUNK, CONV_WIDTH))
        for t in range(CONV_KERNEL):
            a, sh = divmod(off0 + t, SUBLANES)
            r0 = c * CONV_CHUNK + a * SUBLANES
            win = (ubuf_ref[r0:r0 + CONV_CHUNK, :] if sh == 0
                   else ush_ref[sh - 1, r0:r0 + CONV_CHUNK, :])
            acc = acc + win * dww_ref[t:t + 1, :]
        convs.append(acc)
    conv = jnp.concatenate(convs, axis=0)
    ubuf_ref[0:CONV_HALO, :] = ubuf_ref[ts:ts + CONV_HALO, :]
    mu = jnp.mean(conv, axis=-1, keepdims=True)
    xc = conv - mu
    y = xc * lax.rsqrt(jnp.mean(xc * xc, axis=-1, keepdims=True) + EPS) * lnw_ref[...] + lnb_ref[...]
    conv = y * jax.nn.sigmoid(y)

    mixed_a = _rms(attn_ref[...], aonw_ref[...])
    mixed_c = _rms(conv, conw_ref[...])
    x1 = x + _bdot(mixed_a, wout_ref[0:ATTN_WIDTH, :]) + _bdot(mixed_c, wout_ref[ATTN_WIDTH:, :])
    x1_ref[0] = x1
    hf = _rms(x1, fnw_ref[...])
    _store_tiled(hf_ref, hf)

    hf_hi = hf.astype(jnp.bfloat16)
    hf_lo = (hf - hf_hi.astype(jnp.float32)).astype(jnp.bfloat16)
    logits = (jnp.dot(hf_hi, wr_ref[0], preferred_element_type=jnp.float32)
              + jnp.dot(hf_lo, wr_ref[0], preferred_element_type=jnp.float32)
              + jnp.dot(hf_hi, wr_ref[1], preferred_element_type=jnp.float32)) + br_ref[...]

    lanef = lane.astype(jnp.float32)
    big = float(LANES)
    gmask = lane < N_GROUPS
    gmax = jnp.max(jnp.where(gmask, logits, -jnp.inf), axis=-1, keepdims=True)
    gidx = jnp.min(jnp.where(gmask & (logits == gmax), lanef, big), axis=-1, keepdims=True)
    gsum = jnp.sum(jnp.where(gmask, jnp.exp(logits - gmax), 0.0), axis=-1, keepdims=True)
    g_p = 1.0 / gsum
    elo = N_GROUPS + EXPERTS_PER_GROUP * gidx
    emask = (lanef >= elo) & (lanef < elo + EXPERTS_PER_GROUP)
    em1 = jnp.max(jnp.where(emask, logits, -jnp.inf), axis=-1, keepdims=True)
    i1 = jnp.min(jnp.where(emask & (logits == em1), lanef, big), axis=-1, keepdims=True)
    mask2 = emask & (lanef != i1)
    em2 = jnp.max(jnp.where(mask2, logits, -jnp.inf), axis=-1, keepdims=True)
    i2 = jnp.min(jnp.where(mask2 & (logits == em2), lanef, big), axis=-1, keepdims=True)
    esum = jnp.sum(jnp.where(emask, jnp.exp(logits - em1), 0.0), axis=-1, keepdims=True)
    p1 = 1.0 / esum
    p2 = jnp.exp(em2 - em1) / esum
    gate1 = g_p * p1 / (p1 + p2)
    gate2 = g_p * p2 / (p1 + p2)
    e1 = i1 - N_GROUPS
    e2 = i2 - N_GROUPS

    oh1 = lanef == e1
    oh2 = lanef == e2
    onehot = jnp.where(oh1 | oh2, 1.0, 0.0)
    tot = jnp.dot(tri_ref[...], onehot.astype(jnp.bfloat16),
                  preferred_element_type=jnp.float32) + run_ref[0:1, :]
    r1 = jnp.sum(jnp.where(oh1, tot, 0.0), axis=-1, keepdims=True)
    r2 = jnp.sum(jnp.where(oh2, tot, 0.0), axis=-1, keepdims=True)
    run_ref[...] = run_ref[...] + jnp.sum(onehot, axis=0, keepdims=True)
    cnt_ref[...] = run_ref[...]

    ri = jnp.where(lane == 0, e1, jnp.where(lane == 1, e2, jnp.where(lane == 2, r1, r2)))
    ri_ref[0] = ri.astype(jnp.int32)
    rg_ref[0] = jnp.where(lane == 0, gate1, gate2)


def _mix(x, cos_t, sin_t, p):
    bsz, seq, d = x.shape
    ts = min(SEQ_TILE, seq)
    nst = seq // ts
    grid = (bsz, nst)
    tile = lambda last: pl.BlockSpec((1, ts, last), lambda b, s, *_: (b, s, 0))
    full = lambda a: pl.BlockSpec(a.shape, lambda b, s, *_: (0,) * a.ndim)
    weights = [p["anw"], p["win"], p["dww"], p["dwb"], p["lnw"], p["lnb"], p["aonw"], p["conw"],
               p["wout"], p["fnw"], p["wr"], p["br"], p["tri"]]
    out_shape = [
        jax.ShapeDtypeStruct((bsz, seq, d), jnp.float32),
        jax.ShapeDtypeStruct((bsz * seq * SUBLANES, LANES), jnp.float32),
        jax.ShapeDtypeStruct((bsz, seq, LANES), jnp.int32),
        jax.ShapeDtypeStruct((bsz, seq, LANES), jnp.float32),
        jax.ShapeDtypeStruct((SUBLANES, LANES), jnp.float32),
    ]
    out_specs = [tile(d),
                 pl.BlockSpec((ts * SUBLANES, LANES), lambda b, s, *_: (b * nst + s, 0)),
                 tile(LANES), tile(LANES),
                 pl.BlockSpec((SUBLANES, LANES), lambda b, s, *_: (0, 0))]
    grid_spec = pltpu.PrefetchScalarGridSpec(
        num_scalar_prefetch=1, grid=grid,
        in_specs=[tile(d), tile(LANES), tile(LANES)] + [full(w) for w in weights],
        out_specs=out_specs,
        scratch_shapes=[
            pltpu.VMEM((4, ts + WINDOW, LANES), jnp.bfloat16),
            pltpu.VMEM((4, ts + WINDOW, LANES), jnp.bfloat16),
            pltpu.VMEM((ts + CONV_HALO, CONV_WIDTH), jnp.float32),
            pltpu.VMEM((SUBLANES - 1, ts + CONV_HALO - SUBLANES, CONV_WIDTH), jnp.float32),
            pltpu.VMEM((ts, ATTN_WIDTH), jnp.float32),
            pltpu.VMEM((SUBLANES, LANES), jnp.float32),
        ])
    return pl.pallas_call(
        _mix_kernel, grid_spec=grid_spec, out_shape=out_shape, name="mix",
        compiler_params=pltpu.CompilerParams(
            dimension_semantics=("arbitrary", "arbitrary"),
            vmem_limit_bytes=VMEM_LIMIT_BYTES),
    )(p["sinks"], x, cos_t, sin_t, *weights)


def _dispatch_kernel(dst_ref, hf_hbm, xs_hbm, sem_ref):
    i = pl.program_id(0)
    n = pl.num_programs(0)
    td = dst_ref.shape[2] // TOP_K
    slot = i % 2

    def copies(sl, start):
        for q0 in range(0, TOP_K * td, INDEX_BATCH):
            qs = range(q0, q0 + INDEX_BATCH)
            idx = [dst_ref[0, 0, q] if start else 0 for q in qs]
            for q, ix in zip(qs, idx):
                cp = pltpu.make_async_copy(hf_hbm.at[i * td + q // TOP_K], xs_hbm.at[ix],
                                           sem_ref.at[sl])
                cp.start() if start else cp.wait()

    copies(slot, True)

    @pl.when(i > 0)
    def _():
        copies(1 - slot, False)

    @pl.when(i == n - 1)
    def _():
        copies(slot, False)


def _dispatch(hf3, dest):
    n = hf3.shape[0]
    td = min(DISPATCH_TILE, n)
    nt = n // td
    return pl.pallas_call(
        _dispatch_kernel, grid=(nt,), name="dispatch",
        in_specs=[pl.BlockSpec((1, 1, TOP_K * td), lambda i: (i, 0, 0), memory_space=pltpu.SMEM),
                  pl.BlockSpec(memory_space=pl.ANY)],
        out_specs=pl.BlockSpec(memory_space=pl.ANY),
        out_shape=jax.ShapeDtypeStruct((TOP_K * n, SUBLANES, LANES), jnp.float32),
        scratch_shapes=[pltpu.SemaphoreType.DMA((2,))],
        compiler_params=pltpu.CompilerParams(dimension_semantics=("arbitrary",)),
    )(dest.reshape(nt, 1, TOP_K * td), hf3)


def _expert_kernel(blk_ref, exp_ref, lo_ref, hi_ref, first_ref, nitems_ref,
                   xs_ref, wg_ref, wu_ref, wd_ref, ys_ref):
    i = pl.program_id(0)
    bm = xs_ref.shape[0] // SUBLANES

    @pl.when(i < nitems_ref[0])
    def _():
        xb = _load_tiled(xs_ref, 0, bm).astype(jnp.bfloat16)
        hg = jnp.dot(xb, wg_ref[0], preferred_element_type=jnp.float32)
        hu = jnp.dot(xb, wu_ref[0], preferred_element_type=jnp.float32)
        h = hg * jax.nn.sigmoid(hg) * hu
        y = jnp.dot(h.astype(jnp.bfloat16), wd_ref[0], preferred_element_type=jnp.float32)
        row = lax.broadcasted_iota(jnp.int32, (bm, 1), 0)
        y = jnp.where((row >= lo_ref[i]) & (row < hi_ref[i]), y, 0.0)

        @pl.when(first_ref[i] == 1)
        def _():
            _store_tiled(ys_ref, y)

        @pl.when(first_ref[i] == 0)
        def _():
            _store_tiled(ys_ref, _load_tiled(ys_ref, 0, bm) + y)


def _experts(xs2, items, wg, wu, wd, bm):
    rows = xs2.shape[0] // SUBLANES
    ni = items[0].shape[0]
    d, de = wg.shape[1], wg.shape[2]
    blk = lambda i, b, e, *_: (b[i], 0)
    wsp = lambda shape: pl.BlockSpec(shape, lambda i, b, e, *_: (e[i], 0, 0))
    grid_spec = pltpu.PrefetchScalarGridSpec(
        num_scalar_prefetch=6, grid=(ni,),
        in_specs=[pl.BlockSpec((bm * SUBLANES, LANES), blk),
                  wsp((1, d, de)), wsp((1, d, de)), wsp((1, de, d))],
        out_specs=pl.BlockSpec((bm * SUBLANES, LANES), blk))
    return pl.pallas_call(
        _expert_kernel, grid_spec=grid_spec, name="experts",
        out_shape=jax.ShapeDtypeStruct((rows * SUBLANES, LANES), jnp.float32),
        compiler_params=pltpu.CompilerParams(
            dimension_semantics=("arbitrary",), vmem_limit_bytes=VMEM_LIMIT_BYTES),
    )(*items, xs2, wg, wu, wd)


def _combine_kernel(dst_ref, dstn_ref, x1_ref, rg_ref, fw_ref, ys_hbm, o_ref, buf_ref, sem_ref):
    i = pl.program_id(0)
    n = pl.num_programs(0)
    tc = x1_ref.shape[0]
    slot = i % 2

    def gather(idx_ref, sl, start):
        for r0 in range(0, TOP_K * tc, INDEX_BATCH):
            rs = range(r0, r0 + INDEX_BATCH)
            idx = [idx_ref[0, 0, r] if start else 0 for r in rs]
            for r, ix in zip(rs, idx):
                cp = pltpu.make_async_copy(ys_hbm.at[ix],
                                           buf_ref.at[sl, pl.ds(SUBLANES * r, SUBLANES), :],
                                           sem_ref.at[sl])
                cp.start() if start else cp.wait()

    @pl.when(i == 0)
    def _():
        gather(dst_ref, 0, True)

    @pl.when(i + 1 < n)
    def _():
        gather(dstn_ref, 1 - slot, True)

    gather(dst_ref, slot, False)
    g = rg_ref[...]
    cur = buf_ref.at[slot]
    y = (_load_tiled(cur, 0, tc) * g[:, 0:1]
         + _load_tiled(cur, SUBLANES * tc, tc) * g[:, 1:2])
    o_ref[...] = _rms(x1_ref[...] + y, fw_ref[...])


def _combine(x1, rg, ys3, dest, fw):
    n, d = x1.shape
    tc = min(COMBINE_TILE, n)
    nt = n // tc
    d3 = dest.reshape(nt, tc, TOP_K).transpose(0, 2, 1).reshape(nt, 1, TOP_K * tc)
    return pl.pallas_call(
        _combine_kernel, grid=(nt,), name="combine",
        in_specs=[
            pl.BlockSpec((1, 1, TOP_K * tc), lambda i: (i, 0, 0), memory_space=pltpu.SMEM),
            pl.BlockSpec((1, 1, TOP_K * tc), lambda i: (jnp.minimum(i + 1, nt - 1), 0, 0),
                         memory_space=pltpu.SMEM),
            pl.BlockSpec((tc, d), lambda i: (i, 0)),
            pl.BlockSpec((tc, LANES), lambda i: (i, 0)),
            pl.BlockSpec((1, d), lambda i: (0, 0)),
            pl.BlockSpec(memory_space=pl.ANY),
        ],
        out_specs=pl.BlockSpec((tc, d), lambda i: (i, 0)),
        out_shape=jax.ShapeDtypeStruct((n, d), jnp.float32),
        scratch_shapes=[pltpu.VMEM((2, TOP_K * tc * SUBLANES, LANES), jnp.float32),
                        pltpu.SemaphoreType.DMA((2,))],
        compiler_params=pltpu.CompilerParams(
            dimension_semantics=("arbitrary",), vmem_limit_bytes=VMEM_LIMIT_BYTES),
    )(d3, d3, x1, rg, fw, ys3)


def _rope_tables(positions):
    half = ROT_DIM // 2
    inv_freq = jnp.power(ROPE_THETA, -jnp.arange(half, dtype=jnp.float32) * 2.0 / ROT_DIM)
    ang = positions.astype(jnp.float32)[..., None] * inv_freq
    j = jnp.arange(LANES) % HEAD_DIM
    rot = j < ROT_DIM
    cos_t = jnp.where(rot, jnp.take(jnp.cos(ang), j % half, axis=-1), 1.0)
    sin_t = jnp.where(rot, jnp.take(jnp.sin(ang), j % half, axis=-1), 0.0)
    sin_t = jnp.where(j < half, -sin_t, sin_t)
    return cos_t, sin_t


def _work_items(counts, bm, ni):
    end = jnp.cumsum(counts)
    start = end - counts
    fb = start // bm
    nblk = jnp.where(counts > 0, (end - 1) // bm - fb + 1, 0)
    iend = jnp.cumsum(nblk)
    nitems = iend[-1]
    i = jnp.minimum(jnp.arange(ni, dtype=jnp.int32), nitems - 1)
    e = jnp.sum((iend[None, :] <= i[:, None]).astype(jnp.int32), axis=1)
    e = jnp.minimum(e, N_EXPERTS - 1)
    blk = fb[e] + i - (iend[e] - nblk[e])
    lo = jnp.maximum(start[e], blk * bm) - blk * bm
    hi = jnp.minimum(end[e], (blk + 1) * bm) - blk * bm
    first = jnp.concatenate([jnp.ones((1,), jnp.int32),
                             (blk[1:] != blk[:-1]).astype(jnp.int32)])
    as32 = lambda a: a.astype(jnp.int32)
    return start, (as32(blk), as32(e), as32(lo), as32(hi), first, as32(nitems).reshape(1))


def _layer(x, cos_t, sin_t, l, attn_norm_w, w_in, attn_sinks, conv_dw_w, conv_dw_b, conv_ln_w,
           conv_ln_b, attn_out_norm_w, conv_out_norm_w, w_out, ffn_norm_w, router_group_w,
           router_group_b, router_expert_w, router_expert_b, w_gate, w_up, w_down, out_norm_w):
    bsz, seq, d = x.shape
    n = bsz * seq
    assert d == SUBLANES * LANES, "token-tiled rows assume a 1024-wide model"
    bf = jnp.bfloat16
    row = lambda a: a.reshape(1, -1)
    wr = jnp.zeros((d, LANES), jnp.float32)
    wr = wr.at[:, :N_GROUPS].set(router_group_w[l]).at[:, N_GROUPS:N_GROUPS + N_EXPERTS].set(
        router_expert_w[l])
    wr_hi = wr.astype(bf)
    wr_lo = (wr - wr_hi.astype(jnp.float32)).astype(bf)
    br = jnp.zeros((1, LANES), jnp.float32)
    br = br.at[0, :N_GROUPS].set(router_group_b[l]).at[0, N_GROUPS:N_GROUPS + N_EXPERTS].set(
        router_expert_b[l])
    ts = min(SEQ_TILE, seq)
    params = dict(
        sinks=attn_sinks[l], anw=row(attn_norm_w[l]), win=w_in[l].astype(bf), dww=conv_dw_w[l],
        dwb=row(conv_dw_b[l]), lnw=row(conv_ln_w[l]), lnb=row(conv_ln_b[l]),
        aonw=row(attn_out_norm_w[l]), conw=row(conv_out_norm_w[l]), wout=w_out[l].astype(bf),
        fnw=row(ffn_norm_w[l]), wr=jnp.stack([wr_hi, wr_lo]), br=br,
        tri=jnp.tri(ts, k=-1, dtype=bf))
    x1, hf2, ri, rg, cnt = _mix(x, cos_t, sin_t, params)

    bm = min(EXPERT_BLOCK, n * TOP_K)
    ni = (n * TOP_K) // bm + N_EXPERTS - 1
    counts = cnt[0, :N_EXPERTS].astype(jnp.int32)
    start, items = _work_items(counts, bm, ni)
    ri2 = ri.reshape(n, LANES)
    dest = start[ri2[:, 0:TOP_K]] + ri2[:, TOP_K:2 * TOP_K]

    xs3 = _dispatch(hf2.reshape(n, SUBLANES, LANES), dest)
    ys2 = _experts(xs3.reshape(TOP_K * n * SUBLANES, LANES), items,
                   w_gate[l].astype(bf), w_up[l].astype(bf), w_down[l].astype(bf), bm)
    out = _combine(x1.reshape(n, d), rg.reshape(n, LANES),
                   ys2.reshape(TOP_K * n, SUBLANES, LANES), dest, row(out_norm_w))
    return out.reshape(bsz, seq, d)


def kernel(x, positions, attn_norm_w, w_in, attn_sinks, conv_dw_w, conv_dw_b, conv_ln_w, conv_ln_b, attn_out_norm_w, conv_out_norm_w, w_out, ffn_norm_w, router_group_w, router_group_b, router_expert_w, router_expert_b, w_gate, w_up, w_down, final_norm_w):
    depth = attn_norm_w.shape[0]
    assert depth == 1, "the combine kernel applies the final norm, so exactly one layer is fused"
    cos_t, sin_t = _rope_tables(positions)
    return _layer(x, cos_t, sin_t, 0, attn_norm_w, w_in, attn_sinks, conv_dw_w, conv_dw_b,
                  conv_ln_w, conv_ln_b, attn_out_norm_w, conv_out_norm_w, w_out, ffn_norm_w,
                  router_group_w, router_group_b, router_expert_w, router_expert_b,
                  w_gate, w_up, w_down, final_norm_w)
```

```python
import jax
import jax.numpy as jnp
from jax import lax
from jax.experimental import pallas as pl
from jax.experimental.pallas import tpu as pltpu

HEAD_DIM = 64
N_Q_HEADS = 8
N_KV_HEADS = 2
ATTN_WIDTH = N_Q_HEADS * HEAD_DIM
KV_WIDTH = N_KV_HEADS * HEAD_DIM
CONV_WIDTH = 512
CONV_KERNEL = 31
WINDOW = 128
ROPE_THETA = 500000.0
ROT_DIM = HEAD_DIM // 4
N_GROUPS = 4
EXPERTS_PER_GROUP = 8
N_EXPERTS = N_GROUPS * EXPERTS_PER_GROUP
TOP_K = 2
EPS = 1e-5

LANES = 128
SUBLANES = 8
VMEM_LIMIT_BYTES = 56 * 1024 * 1024

SEQ_TILE = 512
CONV_HALO = 32
CONV_CHUNK = 64
EXPERT_BLOCK = 512
INDEX_BATCH = 16
NEG_BIG = -1e30

Q_END = ATTN_WIDTH
K_END = Q_END + KV_WIDTH
V_END = K_END + KV_WIDTH
CA_END = V_END + CONV_WIDTH
IN_COLS = CA_END + CONV_WIDTH


def _rms(x, w):
    return x * lax.rsqrt(jnp.mean(x * x, axis=-1, keepdims=True) + EPS) * w


def _bdot(a, b):
    return jnp.dot(a.astype(jnp.bfloat16), b.astype(jnp.bfloat16),
                   preferred_element_type=jnp.float32)


def _bdot_t(a, b):
    return lax.dot_general(a.astype(jnp.bfloat16), b.astype(jnp.bfloat16),
                           (((1,), (1,)), ((), ())), preferred_element_type=jnp.float32)


def _load_tiled(ref, base, rows):
    d = SUBLANES * LANES
    return jnp.concatenate(
        [ref[pl.ds(base + j, rows, stride=SUBLANES), :] for j in range(d // LANES)], axis=1)


def _store_tiled(ref, val):
    for j in range(val.shape[1] // LANES):
        ref[pl.ds(j, val.shape[0], stride=SUBLANES), :] = val[:, j * LANES:(j + 1) * LANES]


def _mix_kernel(sinks_ref,
                x_ref, cs_ref, sel_ref, anw_ref, win_ref, dww_ref, dwb_ref, lnw_ref, lnb_ref,
                aonw_ref, conw_ref, wout_ref, fnw_ref, wr_ref, br_ref, tri_ref,
                x1_ref, hf_ref, ri_ref, rg_ref, cnt_ref,
                kd_ref, vd_ref, ubuf_ref, ush_ref, attn_ref, run_ref):
    b = pl.program_id(0)
    s = pl.program_id(1)
    ts = x_ref.shape[1]
    nblk = ts // WINDOW
    lane = lax.broadcasted_iota(jnp.int32, (ts, LANES), 1)

    @pl.when(s == 0)
    def _():
        kd_ref[:, 0:WINDOW, :] = jnp.zeros((4, WINDOW, LANES), jnp.bfloat16)
        vd_ref[:, 0:WINDOW, :] = jnp.zeros((4, WINDOW, LANES), jnp.bfloat16)
        ubuf_ref[0:CONV_HALO, :] = jnp.zeros((CONV_HALO, CONV_WIDTH), jnp.float32)

    @pl.when((s == 0) & (b == 0))
    def _():
        run_ref[...] = jnp.zeros_like(run_ref)

    x = x_ref[0]
    hn = _rms(x, anw_ref[...])
    proj = _bdot(hn, win_ref[...])

    cs = cs_ref[0]
    cs_hi = cs.astype(jnp.bfloat16)
    cs_r = cs - cs_hi.astype(jnp.float32)
    cs_mid = cs_r.astype(jnp.bfloat16)
    cs_lo = (cs_r - cs_mid.astype(jnp.float32)).astype(jnp.bfloat16)
    tab = jnp.dot(jnp.concatenate([cs_hi, cs_mid, cs_lo], axis=1), sel_ref[...],
                  preferred_element_type=jnp.float32)
    rotary = (lane % HEAD_DIM) < ROT_DIM
    cosv = jnp.where(rotary, tab[:, 0:LANES], 1.0)
    sinv = tab[:, LANES:2 * LANES]
    first_half = (lane % HEAD_DIM) < (ROT_DIM // 2)

    def rope(t):
        partner = jnp.where(first_half, pltpu.roll(t, LANES - ROT_DIM // 2, 1),
                            pltpu.roll(t, ROT_DIM // 2, 1))
        return t * cosv + partner * sinv

    k = rope(proj[:, Q_END:K_END])
    v = proj[:, K_END:V_END]
    low = lane < HEAD_DIM
    for src, dst in ((k, kd_ref), (v, vd_ref)):
        rolled = pltpu.roll(src, HEAD_DIM, 1)
        zero = jnp.zeros_like(src)
        dst[0, WINDOW:, :] = jnp.where(low, src, zero).astype(jnp.bfloat16)
        dst[1, WINDOW:, :] = jnp.where(low, zero, rolled).astype(jnp.bfloat16)
        dst[2, WINDOW:, :] = jnp.where(low, rolled, zero).astype(jnp.bfloat16)
        dst[3, WINDOW:, :] = jnp.where(low, zero, src).astype(jnp.bfloat16)

    qi = lax.broadcasted_iota(jnp.int32, (WINDOW, 2 * WINDOW), 0)
    kj = lax.broadcasted_iota(jnp.int32, (WINDOW, 2 * WINDOW), 1)
    dist = qi + WINDOW - kj
    band = (dist >= 0) & (dist < WINDOW)
    first_lo = jnp.where(s == 0, WINDOW, 0)
    lane_q = lax.broadcasted_iota(jnp.int32, (WINDOW, LANES), 1)
    for p in range(N_Q_HEADS // 2):
        g = p // 2
        qs = rope(proj[:, p * LANES:(p + 1) * LANES]) * (HEAD_DIM ** -0.5)
        for j in range(nblk):
            rows = slice(j * WINDOW, (j + 1) * WINDOW)
            keys = slice(j * WINDOW, (j + 2) * WINDOW)
            valid = (band & (kj >= first_lo)) if j == 0 else band
            qb = qs[rows]
            out = None
            inv = []
            for half in range(2):
                sink = sinks_ref[2 * p + half]
                sc = _bdot_t(qb, kd_ref[2 * g + half, keys, :])
                sc = jnp.where(valid, sc, NEG_BIG)
                m = jnp.maximum(jnp.max(sc, axis=-1, keepdims=True), sink)
                e = jnp.exp(sc - m)
                inv.append(1.0 / (jnp.sum(e, axis=-1, keepdims=True) + jnp.exp(sink - m)))
                o = _bdot(e, vd_ref[2 * g + half, keys, :])
                out = o if out is None else out + o
            attn_ref[rows, p * LANES:(p + 1) * LANES] = out * jnp.where(
                lane_q < HEAD_DIM, inv[0], inv[1])
    for c in range(4):
        kd_ref[c, 0:WINDOW, :] = kd_ref[c, ts:ts + WINDOW, :]
        vd_ref[c, 0:WINDOW, :] = vd_ref[c, ts:ts + WINDOW, :]

    u = proj[:, V_END:CA_END] * jax.nn.sigmoid(proj[:, CA_END:IN_COLS])
    ubuf_ref[CONV_HALO:, :] = u
    span = ush_ref.shape[1]
    for sh in range(1, SUBLANES):
        ush_ref[sh - 1] = ubuf_ref[sh:sh + span, :]
    dwb = dwb_ref[...]
    off0 = CONV_HALO - (CONV_KERNEL - 1)
    convs = []
    for c in range(ts // CONV_CHUNK):
        acc = jnp.broadcast_to(dwb, (CONV_CHUNK, CONV_WIDTH))
        for t in range(CONV_KERNEL):
            a, sh = divmod(off0 + t, SUBLANES)
            r0 = c * CONV_CHUNK + a * SUBLANES
            win = (ubuf_ref[r0:r0 + CONV_CHUNK, :] if sh == 0
                   else ush_ref[sh - 1, r0:r0 + CONV_CHUNK, :])
            acc = acc + win * dww_ref[t:t + 1, :]
        convs.append(acc)
    conv = jnp.concatenate(convs, axis=0)
    ubuf_ref[0:CONV_HALO, :] = ubuf_ref[ts:ts + CONV_HALO, :]
    mu = jnp.mean(conv, axis=-1, keepdims=True)
    xc = conv - mu
    y = xc * lax.rsqrt(jnp.mean(xc * xc, axis=-1, keepdims=True) + EPS) * lnw_ref[...] + lnb_ref[...]
    conv = y * jax.nn.sigmoid(y)

    mixed_a = _rms(attn_ref[...], aonw_ref[...])
    mixed_c = _rms(conv, conw_ref[...])
    x1 = x + _bdot(mixed_a, wout_ref[0:ATTN_WIDTH, :]) + _bdot(mixed_c, wout_ref[ATTN_WIDTH:, :])
    x1_ref[0] = x1
    hf = _rms(x1, fnw_ref[...])
    _store_tiled(hf_ref, hf)

    hf_hi = hf.astype(jnp.bfloat16)
    hf_lo = (hf - hf_hi.astype(jnp.float32)).astype(jnp.bfloat16)
    logits = (jnp.dot(hf_hi, wr_ref[0], preferred_element_type=jnp.float32)
              + jnp.dot(hf_lo, wr_ref[0], preferred_element_type=jnp.float32)
              + jnp.dot(hf_hi, wr_ref[1], preferred_element_type=jnp.float32)) + br_ref[...]

    lanef = lane.astype(jnp.float32)
    big = float(LANES)
    gmask = lane < N_GROUPS
    gmax = jnp.max(jnp.where(gmask, logits, -jnp.inf), axis=-1, keepdims=True)
    gidx = jnp.min(jnp.where(gmask & (logits == gmax), lanef, big), axis=-1, keepdims=True)
    gsum = jnp.sum(jnp.where(gmask, jnp.exp(logits - gmax), 0.0), axis=-1, keepdims=True)
    g_p = 1.0 / gsum
    elo = N_GROUPS + EXPERTS_PER_GROUP * gidx
    emask = (lanef >= elo) & (lanef < elo + EXPERTS_PER_GROUP)
    em1 = jnp.max(jnp.where(emask, logits, -jnp.inf), axis=-1, keepdims=True)
    i1 = jnp.min(jnp.where(emask & (logits == em1), lanef, big), axis=-1, keepdims=True)
    mask2 = emask & (lanef != i1)
    em2 = jnp.max(jnp.where(mask2, logits, -jnp.inf), axis=-1, keepdims=True)
    i2 = jnp.min(jnp.where(mask2 & (logits == em2), lanef, big), axis=-1, keepdims=True)
    esum = jnp.sum(jnp.where(emask, jnp.exp(logits - em1), 0.0), axis=-1, keepdims=True)
    p1 = 1.0 / esum
    p2 = jnp.exp(em2 - em1) / esum
    gate1 = g_p * p1 / (p1 + p2)
    gate2 = g_p * p2 / (p1 + p2)
    e1 = i1 - N_GROUPS
    e2 = i2 - N_GROUPS

    oh1 = lanef == e1
    oh2 = lanef == e2
    onehot = jnp.where(oh1 | oh2, 1.0, 0.0)
    tot = jnp.dot(tri_ref[...], onehot.astype(jnp.bfloat16),
                  preferred_element_type=jnp.float32) + run_ref[0:1, :]
    r1 = jnp.sum(jnp.where(oh1, tot, 0.0), axis=-1, keepdims=True)
    r2 = jnp.sum(jnp.where(oh2, tot, 0.0), axis=-1, keepdims=True)
    run_ref[...] = run_ref[...] + jnp.sum(onehot, axis=0, keepdims=True)
    cnt_ref[...] = run_ref[...]

    ri = jnp.where(lane == 0, e1, jnp.where(lane == 1, e2, jnp.where(lane == 2, r1, r2)))
    ri_ref[0] = ri.T[0:SUBLANES, :].astype(jnp.int32)
    rg_ref[0] = jnp.where(lane == 0, gate1, gate2)


def _mix(x, cs, p):
    bsz, seq, d = x.shape
    ts = min(SEQ_TILE, seq)
    nst = seq // ts
    grid = (bsz, nst)
    tile = lambda last: pl.BlockSpec((1, ts, last), lambda b, s, *_: (b, s, 0))
    full = lambda a: pl.BlockSpec(a.shape, lambda b, s, *_: (0,) * a.ndim)
    weights = [p["sel"], p["anw"], p["win"], p["dww"], p["dwb"], p["lnw"], p["lnb"], p["aonw"], p["conw"],
               p["wout"], p["fnw"], p["wr"], p["br"], p["tri"]]
    out_shape = [
        jax.ShapeDtypeStruct((bsz, seq, d), jnp.float32),
        jax.ShapeDtypeStruct((bsz * seq * SUBLANES, LANES), jnp.float32),
        jax.ShapeDtypeStruct((bsz * nst, SUBLANES, ts), jnp.int32),
        jax.ShapeDtypeStruct((bsz, seq, LANES), jnp.float32),
        jax.ShapeDtypeStruct((SUBLANES, LANES), jnp.float32),
    ]
    out_specs = [tile(d),
                 pl.BlockSpec((ts * SUBLANES, LANES), lambda b, s, *_: (b * nst + s, 0)),
                 pl.BlockSpec((1, SUBLANES, ts), lambda b, s, *_: (b * nst + s, 0, 0)),
                 tile(LANES),
                 pl.BlockSpec((SUBLANES, LANES), lambda b, s, *_: (0, 0))]
    grid_spec = pltpu.PrefetchScalarGridSpec(
        num_scalar_prefetch=1, grid=grid,
        in_specs=[tile(d), tile(cs.shape[2])] + [full(w) for w in weights],
        out_specs=out_specs,
        scratch_shapes=[
            pltpu.VMEM((4, ts + WINDOW, LANES), jnp.bfloat16),
            pltpu.VMEM((4, ts + WINDOW, LANES), jnp.bfloat16),
            pltpu.VMEM((ts + CONV_HALO, CONV_WIDTH), jnp.float32),
            pltpu.VMEM((SUBLANES - 1, ts + CONV_HALO - SUBLANES, CONV_WIDTH), jnp.float32),
            pltpu.VMEM((ts, ATTN_WIDTH), jnp.float32),
            pltpu.VMEM((SUBLANES, LANES), jnp.float32),
        ])
    return pl.pallas_call(
        _mix_kernel, grid_spec=grid_spec, out_shape=out_shape, name="mix",
        compiler_params=pltpu.CompilerParams(
            dimension_semantics=("arbitrary", "arbitrary"),
            vmem_limit_bytes=VMEM_LIMIT_BYTES),
    )(p["sinks"], x, cs, *weights)


def _dispatch_kernel(dst_ref, hf_ref, xs_hbm, sem_ref):
    td = dst_ref.shape[2] // TOP_K

    def copies(start):
        for q0 in range(0, TOP_K * td, INDEX_BATCH):
            qs = range(q0, q0 + INDEX_BATCH)
            idx = [dst_ref[0, 0, q] if start else 0 for q in qs]
            for q, ix in zip(qs, idx):
                cp = pltpu.make_async_copy(hf_ref.at[q % td], xs_hbm.at[ix], sem_ref.at[0])
                cp.start() if start else cp.wait()

    copies(True)
    copies(False)


def _dispatch(hf3, d3):
    n = hf3.shape[0]
    nt = d3.shape[0]
    td = n // nt
    return pl.pallas_call(
        _dispatch_kernel, grid=(nt,), name="dispatch",
        in_specs=[pl.BlockSpec((1, 1, TOP_K * td), lambda i: (i, 0, 0), memory_space=pltpu.SMEM),
                  pl.BlockSpec((td, SUBLANES, LANES), lambda i: (i, 0, 0))],
        out_specs=pl.BlockSpec(memory_space=pl.ANY),
        out_shape=jax.ShapeDtypeStruct((TOP_K * n, SUBLANES, LANES), jnp.float32),
        scratch_shapes=[pltpu.SemaphoreType.DMA((1,))],
        compiler_params=pltpu.CompilerParams(dimension_semantics=("arbitrary",)),
    )(d3, hf3)


def _expert_kernel(blk_ref, exp_ref, lo_ref, hi_ref, first_ref, newexp_ref, nitems_ref,
                   xs_ref, wg_ref, wu_ref, wd_ref, ys_ref, wgb_ref, wub_ref, wdb_ref):
    i = pl.program_id(0)
    bm = xs_ref.shape[0] // SUBLANES

    @pl.when((i < nitems_ref[0]) & (newexp_ref[i] == 1))
    def _():
        wgb_ref[...] = wg_ref[0].astype(jnp.bfloat16)
        wub_ref[...] = wu_ref[0].astype(jnp.bfloat16)
        wdb_ref[...] = wd_ref[0].astype(jnp.bfloat16)

    @pl.when(i < nitems_ref[0])
    def _():
        xb = _load_tiled(xs_ref, 0, bm).astype(jnp.bfloat16)
        hg = jnp.dot(xb, wgb_ref[...], preferred_element_type=jnp.float32)
        hu = jnp.dot(xb, wub_ref[...], preferred_element_type=jnp.float32)
        h = hg * jax.nn.sigmoid(hg) * hu
        y = jnp.dot(h.astype(jnp.bfloat16), wdb_ref[...], preferred_element_type=jnp.float32)
        row = lax.broadcasted_iota(jnp.int32, (bm, 1), 0)
        y = jnp.where((row >= lo_ref[i]) & (row < hi_ref[i]), y, 0.0)

        @pl.when(first_ref[i] == 1)
        def _():
            _store_tiled(ys_ref, y)

        @pl.when(first_ref[i] == 0)
        def _():
            _store_tiled(ys_ref, _load_tiled(ys_ref, 0, bm) + y)


def _experts(xs2, items, wg, wu, wd, bm):
    rows = xs2.shape[0] // SUBLANES
    ni = items[0].shape[0]
    d, de = wg.shape[1], wg.shape[2]
    blk = lambda i, b, e, *_: (b[i], 0)
    wsp = lambda shape: pl.BlockSpec(shape, lambda i, b, e, *_: (e[i], 0, 0))
    grid_spec = pltpu.PrefetchScalarGridSpec(
        num_scalar_prefetch=len(items), grid=(ni,),
        in_specs=[pl.BlockSpec((bm * SUBLANES, LANES), blk),
                  wsp((1, d, de)), wsp((1, d, de)), wsp((1, de, d))],
        out_specs=pl.BlockSpec((bm * SUBLANES, LANES), blk),
        scratch_shapes=[pltpu.VMEM((d, de), jnp.bfloat16), pltpu.VMEM((d, de), jnp.bfloat16),
                        pltpu.VMEM((de, d), jnp.bfloat16)])
    return pl.pallas_call(
        _expert_kernel, grid_spec=grid_spec, name="experts",
        out_shape=jax.ShapeDtypeStruct((rows * SUBLANES, LANES), jnp.float32),
        compiler_params=pltpu.CompilerParams(
            dimension_semantics=("arbitrary",), vmem_limit_bytes=VMEM_LIMIT_BYTES),
    )(*items, xs2, wg, wu, wd)


def _combine_kernel(dst_ref, dstn_ref, x1_ref, rg_ref, fw_ref, ys_hbm, o_ref, buf_ref, sem_ref):
    i = pl.program_id(0)
    n = pl.num_programs(0)
    tc = x1_ref.shape[0]
    slot = i % 2

    def gather(idx_ref, sl, start):
        for r0 in range(0, TOP_K * tc, INDEX_BATCH):
            rs = range(r0, r0 + INDEX_BATCH)
            idx = [idx_ref[0, 0, r] if start else 0 for r in rs]
            for r, ix in zip(rs, idx):
                cp = pltpu.make_async_copy(ys_hbm.at[ix],
                                           buf_ref.at[sl, pl.ds(SUBLANES * r, SUBLANES), :],
                                           sem_ref.at[sl])
                cp.start() if start else cp.wait()

    @pl.when(i == 0)
    def _():
        gather(dst_ref, 0, True)

    @pl.when(i + 1 < n)
    def _():
        gather(dstn_ref, 1 - slot, True)

    gather(dst_ref, slot, False)
    g = rg_ref[...]
    cur = buf_ref.at[slot]
    y = (_load_tiled(cur, 0, tc) * g[:, 0:1]
         + _load_tiled(cur, SUBLANES * tc, tc) * g[:, 1:2])
    o_ref[...] = _rms(x1_ref[...] + y, fw_ref[...])


def _combine(x1, rg, ys3, d3, fw):
    n, d = x1.shape
    nt = d3.shape[0]
    tc = n // nt
    return pl.pallas_call(
        _combine_kernel, grid=(nt,), name="combine",
        in_specs=[
            pl.BlockSpec((1, 1, TOP_K * tc), lambda i: (i, 0, 0), memory_space=pltpu.SMEM),
            pl.BlockSpec((1, 1, TOP_K * tc), lambda i: (jnp.minimum(i + 1, nt - 1), 0, 0),
                         memory_space=pltpu.SMEM),
            pl.BlockSpec((tc, d), lambda i: (i, 0)),
            pl.BlockSpec((tc, LANES), lambda i: (i, 0)),
            pl.BlockSpec((1, d), lambda i: (0, 0)),
            pl.BlockSpec(memory_space=pl.ANY),
        ],
        out_specs=pl.BlockSpec((tc, d), lambda i: (i, 0)),
        out_shape=jax.ShapeDtypeStruct((n, d), jnp.float32),
        scratch_shapes=[pltpu.VMEM((2, TOP_K * tc * SUBLANES, LANES), jnp.float32),
                        pltpu.SemaphoreType.DMA((2,))],
        compiler_params=pltpu.CompilerParams(
            dimension_semantics=("arbitrary",), vmem_limit_bytes=VMEM_LIMIT_BYTES),
    )(d3, d3, x1, rg, fw, ys3)


def _rope_tables(positions):
    half = ROT_DIM // 2
    inv_freq = jnp.power(ROPE_THETA, -jnp.arange(half, dtype=jnp.float32) * 2.0 / ROT_DIM)
    ang = positions.astype(jnp.float32)[..., None] * inv_freq
    cs = jnp.concatenate([jnp.cos(ang), jnp.sin(ang)], axis=-1)
    j = jnp.arange(LANES) % HEAD_DIM
    f = jnp.arange(half)[:, None]
    hit = (j[None, :] < ROT_DIM) & ((j[None, :] % half) == f)
    sign = jnp.where(j < half, -1.0, 1.0)[None, :]
    zero = jnp.zeros((half, LANES), jnp.float32)
    sel = jnp.concatenate([jnp.concatenate([hit.astype(jnp.float32), zero], axis=1),
                           jnp.concatenate([zero, hit * sign], axis=1)], axis=0)
    return cs, jnp.tile(sel, (3, 1)).astype(jnp.bfloat16)


def _work_items(counts, bm, ni):
    end = jnp.cumsum(counts)
    start = end - counts
    fb = start // bm
    nblk = jnp.where(counts > 0, (end - 1) // bm - fb + 1, 0)
    iend = jnp.cumsum(nblk)
    nitems = iend[-1]
    i = jnp.minimum(jnp.arange(ni, dtype=jnp.int32), nitems - 1)
    e = jnp.sum((iend[None, :] <= i[:, None]).astype(jnp.int32), axis=1)
    e = jnp.minimum(e, N_EXPERTS - 1)
    blk = fb[e] + i - (iend[e] - nblk[e])
    lo = jnp.maximum(start[e], blk * bm) - blk * bm
    hi = jnp.minimum(end[e], (blk + 1) * bm) - blk * bm
    one = jnp.ones((1,), jnp.int32)
    first = jnp.concatenate([one, (blk[1:] != blk[:-1]).astype(jnp.int32)])
    newexp = jnp.concatenate([one, (e[1:] != e[:-1]).astype(jnp.int32)])
    as32 = lambda a: a.astype(jnp.int32)
    return start, (as32(blk), as32(e), as32(lo), as32(hi), first, newexp,
                   as32(nitems).reshape(1))


def _layer(x, cs, sel, l, attn_norm_w, w_in, attn_sinks, conv_dw_w, conv_dw_b, conv_ln_w,
           conv_ln_b, attn_out_norm_w, conv_out_norm_w, w_out, ffn_norm_w, router_group_w,
           router_group_b, router_expert_w, router_expert_b, w_gate, w_up, w_down, out_norm_w):
    bsz, seq, d = x.shape
    n = bsz * seq
    assert d == SUBLANES * LANES, "token-tiled rows assume a 1024-wide model"
    bf = jnp.bfloat16
    row = lambda a: a.reshape(1, -1)
    wr = jnp.zeros((d, LANES), jnp.float32)
    wr = wr.at[:, :N_GROUPS].set(router_group_w[l]).at[:, N_GROUPS:N_GROUPS + N_EXPERTS].set(
        router_expert_w[l])
    wr_hi = wr.astype(bf)
    wr_lo = (wr - wr_hi.astype(jnp.float32)).astype(bf)
    br = jnp.zeros((1, LANES), jnp.float32)
    br = br.at[0, :N_GROUPS].set(router_group_b[l]).at[0, N_GROUPS:N_GROUPS + N_EXPERTS].set(
        router_expert_b[l])
    ts = min(SEQ_TILE, seq)
    params = dict(
        sinks=attn_sinks[l], sel=sel, anw=row(attn_norm_w[l]), win=w_in[l].astype(bf),
        dww=conv_dw_w[l], dwb=row(conv_dw_b[l]), lnw=row(conv_ln_w[l]), lnb=row(conv_ln_b[l]),
        aonw=row(attn_out_norm_w[l]), conw=row(conv_out_norm_w[l]), wout=w_out[l].astype(bf),
        fnw=row(ffn_norm_w[l]), wr=jnp.stack([wr_hi, wr_lo]), br=br,
        tri=jnp.tri(ts, k=-1, dtype=bf))
    x1, hf2, ri, rg, cnt = _mix(x, cs, params)

    bm = min(EXPERT_BLOCK, n * TOP_K)
    ni = (n * TOP_K) // bm + N_EXPERTS - 1
    counts = cnt[0, :N_EXPERTS].astype(jnp.int32)
    start, items = _work_items(counts, bm, ni)
    nt = n // ts
    d3 = (start[ri[:, 0:TOP_K, :]] + ri[:, TOP_K:2 * TOP_K, :]).reshape(nt, 1, TOP_K * ts)

    xs3 = _dispatch(hf2.reshape(n, SUBLANES, LANES), d3)
    ys2 = _experts(xs3.reshape(TOP_K * n * SUBLANES, LANES), items,
                   w_gate[l], w_up[l], w_down[l], bm)
    out = _combine(x1.reshape(n, d), rg.reshape(n, LANES),
                   ys2.reshape(TOP_K * n, SUBLANES, LANES), d3, row(out_norm_w))
    return out.reshape(bsz, seq, d)


def kernel(x, positions, attn_norm_w, w_in, attn_sinks, conv_dw_w, conv_dw_b, conv_ln_w, conv_ln_b, attn_out_norm_w, conv_out_norm_w, w_out, ffn_norm_w, router_group_w, router_group_b, router_expert_w, router_expert_b, w_gate, w_up, w_down, final_norm_w):
    depth = attn_norm_w.shape[0]
    assert depth == 1, "the combine kernel applies the final norm, so exactly one layer is fused"
    cs, sel = _rope_tables(positions)
    return _layer(x, cs, sel, 0, attn_norm_w, w_in, attn_sinks, conv_dw_w, conv_dw_b,
                  conv_ln_w, conv_ln_b, attn_out_norm_w, conv_out_norm_w, w_out, ffn_norm_w,
                  router_group_w, router_group_b, router_expert_w, router_expert_b,
                  w_gate, w_up, w_down, final_norm_w)
```

```python
import jax
import jax.numpy as jnp
from jax import lax
from jax.experimental import pallas as pl
from jax.experimental.pallas import tpu as pltpu

HEAD_DIM = 64
N_Q_HEADS = 8
N_KV_HEADS = 2
ATTN_WIDTH = N_Q_HEADS * HEAD_DIM
KV_WIDTH = N_KV_HEADS * HEAD_DIM
CONV_WIDTH = 512
CONV_KERNEL = 31
WINDOW = 128
ROPE_THETA = 500000.0
ROT_DIM = HEAD_DIM // 4
N_GROUPS = 4
EXPERTS_PER_GROUP = 8
N_EXPERTS = N_GROUPS * EXPERTS_PER_GROUP
TOP_K = 2
EPS = 1e-5

LANES = 128
SUBLANES = 8
VMEM_LIMIT_BYTES = 56 * 1024 * 1024

SEQ_TILE = 512
CONV_HALO = 32
CONV_CHUNK = 64
EXPERT_BLOCK = 512
DISPATCH_SLOTS = 3
INDEX_BATCH = 16
NEG_BIG = -1e30

Q_END = ATTN_WIDTH
K_END = Q_END + KV_WIDTH
V_END = K_END + KV_WIDTH
CA_END = V_END + CONV_WIDTH
IN_COLS = CA_END + CONV_WIDTH


def _rms(x, w):
    return x * lax.rsqrt(jnp.mean(x * x, axis=-1, keepdims=True) + EPS) * w


def _bdot(a, b):
    return jnp.dot(a.astype(jnp.bfloat16), b.astype(jnp.bfloat16),
                   preferred_element_type=jnp.float32)


def _bdot_t(a, b):
    return lax.dot_general(a.astype(jnp.bfloat16), b.astype(jnp.bfloat16),
                           (((1,), (1,)), ((), ())), preferred_element_type=jnp.float32)


def _load_tiled(ref, base, rows):
    d = SUBLANES * LANES
    return jnp.concatenate(
        [ref[pl.ds(base + j, rows, stride=SUBLANES), :] for j in range(d // LANES)], axis=1)


def _store_tiled(ref, val):
    for j in range(val.shape[1] // LANES):
        ref[pl.ds(j, val.shape[0], stride=SUBLANES), :] = val[:, j * LANES:(j + 1) * LANES]


def _mix_kernel(sinks_ref,
                x_ref, cs_ref, sel_ref, anw_ref, win_ref, dww_ref, dwb_ref, lnw_ref, lnb_ref,
                aonw_ref, conw_ref, wout_ref, fnw_ref, wr_ref, br_ref, tri_ref,
                x1_ref, hf_ref, ri_ref, rg_ref, cnt_ref,
                kd_ref, vd_ref, ubuf_ref, ush_ref, attn_ref, run_ref):
    b = pl.program_id(0)
    s = pl.program_id(1)
    ts = x_ref.shape[1]
    nblk = ts // WINDOW
    lane = lax.broadcasted_iota(jnp.int32, (ts, LANES), 1)

    @pl.when(s == 0)
    def _():
        kd_ref[:, 0:WINDOW, :] = jnp.zeros((4, WINDOW, LANES), jnp.bfloat16)
        vd_ref[:, 0:WINDOW, :] = jnp.zeros((4, WINDOW, LANES), jnp.bfloat16)
        ubuf_ref[0:CONV_HALO, :] = jnp.zeros((CONV_HALO, CONV_WIDTH), jnp.float32)

    @pl.when((s == 0) & (b == 0))
    def _():
        run_ref[...] = jnp.zeros_like(run_ref)

    x = x_ref[0]
    hn = _rms(x, anw_ref[...]).astype(jnp.bfloat16)
    proj_c = _bdot(hn, win_ref[:, V_END:IN_COLS])

    u = proj_c[:, 0:CONV_WIDTH] * jax.nn.sigmoid(proj_c[:, CONV_WIDTH:2 * CONV_WIDTH])
    ubuf_ref[CONV_HALO:, :] = u
    span = ush_ref.shape[1]
    for sh in range(1, SUBLANES):
        ush_ref[sh - 1] = ubuf_ref[sh:sh + span, :]
    dwb = dwb_ref[...]
    off0 = CONV_HALO - (CONV_KERNEL - 1)
    convs = []
    for c in range(ts // CONV_CHUNK):
        acc = jnp.broadcast_to(dwb, (CONV_CHUNK, CONV_WIDTH))
        for t in range(CONV_KERNEL):
            a, sh = divmod(off0 + t, SUBLANES)
            r0 = c * CONV_CHUNK + a * SUBLANES
            win = (ubuf_ref[r0:r0 + CONV_CHUNK, :] if sh == 0
                   else ush_ref[sh - 1, r0:r0 + CONV_CHUNK, :])
            acc = acc + win * dww_ref[t:t + 1, :]
        convs.append(acc)
    conv = jnp.concatenate(convs, axis=0)
    ubuf_ref[0:CONV_HALO, :] = ubuf_ref[ts:ts + CONV_HALO, :]
    mu = jnp.mean(conv, axis=-1, keepdims=True)
    xc = conv - mu
    y = xc * lax.rsqrt(jnp.mean(xc * xc, axis=-1, keepdims=True) + EPS) * lnw_ref[...] + lnb_ref[...]
    conv = y * jax.nn.sigmoid(y)
    mixed_c = _rms(conv, conw_ref[...]).astype(jnp.bfloat16)

    proj_kv = _bdot(hn, win_ref[:, Q_END:V_END])
    proj_q = _bdot(hn, win_ref[:, 0:Q_END])

    cs = cs_ref[0]
    cs_hi = cs.astype(jnp.bfloat16)
    cs_r = cs - cs_hi.astype(jnp.float32)
    cs_mid = cs_r.astype(jnp.bfloat16)
    cs_lo = (cs_r - cs_mid.astype(jnp.float32)).astype(jnp.bfloat16)
    tab = jnp.dot(jnp.concatenate([cs_hi, cs_mid, cs_lo], axis=1), sel_ref[...],
                  preferred_element_type=jnp.float32)
    rotary = (lane % HEAD_DIM) < ROT_DIM
    cosv = jnp.where(rotary, tab[:, 0:LANES], 1.0)
    sinv = tab[:, LANES:2 * LANES]
    first_half = (lane % HEAD_DIM) < (ROT_DIM // 2)

    def rope(t):
        partner = jnp.where(first_half, pltpu.roll(t, LANES - ROT_DIM // 2, 1),
                            pltpu.roll(t, ROT_DIM // 2, 1))
        return t * cosv + partner * sinv

    k = rope(proj_kv[:, 0:KV_WIDTH])
    v = proj_kv[:, KV_WIDTH:2 * KV_WIDTH]
    low = lane < HEAD_DIM
    for src, dst in ((k, kd_ref), (v, vd_ref)):
        rolled = pltpu.roll(src, HEAD_DIM, 1)
        zero = jnp.zeros_like(src)
        dst[0, WINDOW:, :] = jnp.where(low, src, zero).astype(jnp.bfloat16)
        dst[1, WINDOW:, :] = jnp.where(low, zero, rolled).astype(jnp.bfloat16)
        dst[2, WINDOW:, :] = jnp.where(low, rolled, zero).astype(jnp.bfloat16)
        dst[3, WINDOW:, :] = jnp.where(low, zero, src).astype(jnp.bfloat16)

    qi = lax.broadcasted_iota(jnp.int32, (WINDOW, 2 * WINDOW), 0)
    kj = lax.broadcasted_iota(jnp.int32, (WINDOW, 2 * WINDOW), 1)
    dist = qi + WINDOW - kj
    band = (dist >= 0) & (dist < WINDOW)
    first_lo = jnp.where(s == 0, WINDOW, 0)
    lane_q = lax.broadcasted_iota(jnp.int32, (WINDOW, LANES), 1)
    for p in range(N_Q_HEADS // 2):
        g = p // 2
        qs = rope(proj_q[:, p * LANES:(p + 1) * LANES]) * (HEAD_DIM ** -0.5)
        for j in range(nblk):
            rows = slice(j * WINDOW, (j + 1) * WINDOW)
            keys = slice(j * WINDOW, (j + 2) * WINDOW)
            valid = (band & (kj >= first_lo)) if j == 0 else band
            qb = qs[rows]
            out = None
            inv = []
            for half in range(2):
                sink = sinks_ref[2 * p + half]
                sc = _bdot_t(qb, kd_ref[2 * g + half, keys, :])
                sc = jnp.where(valid, sc, NEG_BIG)
                m = jnp.maximum(jnp.max(sc, axis=-1, keepdims=True), sink)
                e = jnp.exp(sc - m)
                inv.append(1.0 / (jnp.sum(e, axis=-1, keepdims=True) + jnp.exp(sink - m)))
                o = _bdot(e, vd_ref[2 * g + half, keys, :])
                out = o if out is None else out + o
            attn_ref[rows, p * LANES:(p + 1) * LANES] = out * jnp.where(
                lane_q < HEAD_DIM, inv[0], inv[1])
    for c in range(4):
        kd_ref[c, 0:WINDOW, :] = kd_ref[c, ts:ts + WINDOW, :]
        vd_ref[c, 0:WINDOW, :] = vd_ref[c, ts:ts + WINDOW, :]

    mixed_a = _rms(attn_ref[...], aonw_ref[...])
    x1 = x + _bdot(mixed_a, wout_ref[0:ATTN_WIDTH, :]) + _bdot(mixed_c, wout_ref[ATTN_WIDTH:, :])
    x1_ref[0] = x1
    hf = _rms(x1, fnw_ref[...])
    _store_tiled(hf_ref, hf)

    hf_hi = hf.astype(jnp.bfloat16)
    hf_lo = (hf - hf_hi.astype(jnp.float32)).astype(jnp.bfloat16)
    logits = (jnp.dot(hf_hi, wr_ref[0], preferred_element_type=jnp.float32)
              + jnp.dot(hf_lo, wr_ref[0], preferred_element_type=jnp.float32)
              + jnp.dot(hf_hi, wr_ref[1], preferred_element_type=jnp.float32)) + br_ref[...]

    lanef = lane.astype(jnp.float32)
    big = float(LANES)
    gmask = lane < N_GROUPS
    gmax = jnp.max(jnp.where(gmask, logits, -jnp.inf), axis=-1, keepdims=True)
    gidx = jnp.min(jnp.where(gmask & (logits == gmax), lanef, big), axis=-1, keepdims=True)
    gsum = jnp.sum(jnp.where(gmask, jnp.exp(logits - gmax), 0.0), axis=-1, keepdims=True)
    g_p = 1.0 / gsum
    elo = N_GROUPS + EXPERTS_PER_GROUP * gidx
    emask = (lanef >= elo) & (lanef < elo + EXPERTS_PER_GROUP)
    em1 = jnp.max(jnp.where(emask, logits, -jnp.inf), axis=-1, keepdims=True)
    i1 = jnp.min(jnp.where(emask & (logits == em1), lanef, big), axis=-1, keepdims=True)
    mask2 = emask & (lanef != i1)
    em2 = jnp.max(jnp.where(mask2, logits, -jnp.inf), axis=-1, keepdims=True)
    i2 = jnp.min(jnp.where(mask2 & (logits == em2), lanef, big), axis=-1, keepdims=True)
    esum = jnp.sum(jnp.where(emask, jnp.exp(logits - em1), 0.0), axis=-1, keepdims=True)
    p1 = 1.0 / esum
    p2 = jnp.exp(em2 - em1) / esum
    gate1 = g_p * p1 / (p1 + p2)
    gate2 = g_p * p2 / (p1 + p2)
    e1 = i1 - N_GROUPS
    e2 = i2 - N_GROUPS

    oh1 = lanef == e1
    oh2 = lanef == e2
    onehot = jnp.where(oh1 | oh2, 1.0, 0.0)
    tot = jnp.dot(tri_ref[...], onehot.astype(jnp.bfloat16),
                  preferred_element_type=jnp.float32) + run_ref[0:1, :]
    r1 = jnp.sum(jnp.where(oh1, tot, 0.0), axis=-1, keepdims=True)
    r2 = jnp.sum(jnp.where(oh2, tot, 0.0), axis=-1, keepdims=True)
    run_ref[...] = run_ref[...] + jnp.sum(onehot, axis=0, keepdims=True)
    cnt_ref[...] = run_ref[...]

    ri = jnp.where(lane == 0, e1, jnp.where(lane == 1, e2, jnp.where(lane == 2, r1, r2)))
    ri_ref[0] = ri.T[0:SUBLANES, :].astype(jnp.int32)
    rg_ref[0] = jnp.where(lane == 0, gate1, gate2)


def _mix(x, cs, p):
    bsz, seq, d = x.shape
    ts = min(SEQ_TILE, seq)
    nst = seq // ts
    grid = (bsz, nst)
    tile = lambda last: pl.BlockSpec((1, ts, last), lambda b, s, *_: (b, s, 0))
    full = lambda a: pl.BlockSpec(a.shape, lambda b, s, *_: (0,) * a.ndim)
    weights = [p["sel"], p["anw"], p["win"], p["dww"], p["dwb"], p["lnw"], p["lnb"], p["aonw"], p["conw"],
               p["wout"], p["fnw"], p["wr"], p["br"], p["tri"]]
    out_shape = [
        jax.ShapeDtypeStruct((bsz, seq, d), jnp.float32),
        jax.ShapeDtypeStruct((bsz * seq * SUBLANES, LANES), jnp.float32),
        jax.ShapeDtypeStruct((bsz * nst, SUBLANES, ts), jnp.int32),
        jax.ShapeDtypeStruct((bsz, seq, LANES), jnp.float32),
        jax.ShapeDtypeStruct((SUBLANES, LANES), jnp.float32),
    ]
    out_specs = [tile(d),
                 pl.BlockSpec((ts * SUBLANES, LANES), lambda b, s, *_: (b * nst + s, 0)),
                 pl.BlockSpec((1, SUBLANES, ts), lambda b, s, *_: (b * nst + s, 0, 0)),
                 tile(LANES),
                 pl.BlockSpec((SUBLANES, LANES), lambda b, s, *_: (0, 0))]
    grid_spec = pltpu.PrefetchScalarGridSpec(
        num_scalar_prefetch=1, grid=grid,
        in_specs=[tile(d), tile(cs.shape[2])] + [full(w) for w in weights],
        out_specs=out_specs,
        scratch_shapes=[
            pltpu.VMEM((4, ts + WINDOW, LANES), jnp.bfloat16),
            pltpu.VMEM((4, ts + WINDOW, LANES), jnp.bfloat16),
            pltpu.VMEM((ts + CONV_HALO, CONV_WIDTH), jnp.float32),
            pltpu.VMEM((SUBLANES - 1, ts + CONV_HALO - SUBLANES, CONV_WIDTH), jnp.float32),
            pltpu.VMEM((ts, ATTN_WIDTH), jnp.float32),
            pltpu.VMEM((SUBLANES, LANES), jnp.float32),
        ])
    return pl.pallas_call(
        _mix_kernel, grid_spec=grid_spec, out_shape=out_shape, name="mix",
        compiler_params=pltpu.CompilerParams(
            dimension_semantics=("arbitrary", "arbitrary"),
            vmem_limit_bytes=VMEM_LIMIT_BYTES),
    )(p["sinks"], x, cs, *weights)


def _dispatch_kernel(dst_ref, hf_hbm, xs_hbm, buf_ref, in_sem, out_sem):
    i = pl.program_id(0)
    n = pl.num_programs(0)
    td = buf_ref.shape[1]
    nslot = buf_ref.shape[0]

    def tile_in(t):
        return pltpu.make_async_copy(hf_hbm.at[pl.ds(t * td, td)], buf_ref.at[t % nslot],
                                     in_sem.at[t % nslot])

    def rows(t, start):
        src = buf_ref.at[t % nslot]
        for q0 in range(0, TOP_K * td, INDEX_BATCH):
            qs = range(q0, q0 + INDEX_BATCH)
            idx = [dst_ref[0, 0, q] if start else 0 for q in qs]
            for q, ix in zip(qs, idx):
                cp = pltpu.make_async_copy(src.at[q % td], xs_hbm.at[ix], out_sem.at[t % nslot])
                cp.start() if start else cp.wait()

    @pl.when(i == 0)
    def _():
        tile_in(0).start()

    @pl.when((i == 0) & (n > 1))
    def _():
        tile_in(1).start()

    tile_in(i).wait()
    rows(i, True)

    @pl.when(i > 0)
    def _():
        rows(i - 1, False)

    @pl.when(i + 2 < n)
    def _():
        tile_in(i + 2).start()

    @pl.when(i == n - 1)
    def _():
        rows(i, False)


def _dispatch(hf3, d3):
    n = hf3.shape[0]
    nt = d3.shape[0]
    td = n // nt
    return pl.pallas_call(
        _dispatch_kernel, grid=(nt,), name="dispatch",
        in_specs=[pl.BlockSpec((1, 1, TOP_K * td), lambda i: (i, 0, 0), memory_space=pltpu.SMEM),
                  pl.BlockSpec(memory_space=pl.ANY)],
        out_specs=pl.BlockSpec(memory_space=pl.ANY),
        out_shape=jax.ShapeDtypeStruct((TOP_K * n, SUBLANES, LANES), jnp.float32),
        scratch_shapes=[pltpu.VMEM((DISPATCH_SLOTS, td, SUBLANES, LANES), jnp.float32),
                        pltpu.SemaphoreType.DMA((DISPATCH_SLOTS,)),
                        pltpu.SemaphoreType.DMA((DISPATCH_SLOTS,))],
        compiler_params=pltpu.CompilerParams(dimension_semantics=("arbitrary",)),
    )(d3, hf3)


def _expert_kernel(blk_ref, exp_ref, lo_ref, hi_ref, first_ref, newexp_ref, nitems_ref,
                   xs_ref, wg_ref, wu_ref, wd_ref, ys_ref, wgb_ref, wub_ref, wdb_ref):
    i = pl.program_id(0)
    bm = xs_ref.shape[0] // SUBLANES

    @pl.when((i < nitems_ref[0]) & (newexp_ref[i] == 1))
    def _():
        wgb_ref[...] = wg_ref[0].astype(jnp.bfloat16)
        wub_ref[...] = wu_ref[0].astype(jnp.bfloat16)
        wdb_ref[...] = wd_ref[0].astype(jnp.bfloat16)

    @pl.when(i < nitems_ref[0])
    def _():
        xb = _load_tiled(xs_ref, 0, bm).astype(jnp.bfloat16)
        hg = jnp.dot(xb, wgb_ref[...], preferred_element_type=jnp.float32)
        hu = jnp.dot(xb, wub_ref[...], preferred_element_type=jnp.float32)
        h = hg * jax.nn.sigmoid(hg) * hu
        y = jnp.dot(h.astype(jnp.bfloat16), wdb_ref[...], preferred_element_type=jnp.float32)
        row = lax.broadcasted_iota(jnp.int32, (bm, 1), 0)
        y = jnp.where((row >= lo_ref[i]) & (row < hi_ref[i]), y, 0.0)

        @pl.when(first_ref[i] == 1)
        def _():
            _store_tiled(ys_ref, y)

        @pl.when(first_ref[i] == 0)
        def _():
            _store_tiled(ys_ref, _load_tiled(ys_ref, 0, bm) + y)


def _experts(xs2, items, wg, wu, wd, bm):
    rows = xs2.shape[0] // SUBLANES
    ni = items[0].shape[0]
    d, de = wg.shape[1], wg.shape[2]
    blk = lambda i, b, e, *_: (b[i], 0)
    wsp = lambda shape: pl.BlockSpec(shape, lambda i, b, e, *_: (e[i], 0, 0))
    grid_spec = pltpu.PrefetchScalarGridSpec(
        num_scalar_prefetch=len(items), grid=(ni,),
        in_specs=[pl.BlockSpec((bm * SUBLANES, LANES), blk),
                  wsp((1, d, de)), wsp((1, d, de)), wsp((1, de, d))],
        out_specs=pl.BlockSpec((bm * SUBLANES, LANES), blk),
        scratch_shapes=[pltpu.VMEM((d, de), jnp.bfloat16), pltpu.VMEM((d, de), jnp.bfloat16),
                        pltpu.VMEM((de, d), jnp.bfloat16)])
    return pl.pallas_call(
        _expert_kernel, grid_spec=grid_spec, name="experts",
        out_shape=jax.ShapeDtypeStruct((rows * SUBLANES, LANES), jnp.float32),
        compiler_params=pltpu.CompilerParams(
            dimension_semantics=("arbitrary",), vmem_limit_bytes=VMEM_LIMIT_BYTES),
    )(*items, xs2, wg, wu, wd)


def _combine_kernel(dst_ref, dstn_ref, x1_ref, rg_ref, fw_ref, ys_hbm, o_ref, buf_ref, sem_ref):
    i = pl.program_id(0)
    n = pl.num_programs(0)
    tc = x1_ref.shape[0]
    slot = i % 2

    def gather(idx_ref, sl, start):
        for r0 in range(0, TOP_K * tc, INDEX_BATCH):
            rs = range(r0, r0 + INDEX_BATCH)
            idx = [idx_ref[0, 0, r] if start else 0 for r in rs]
            for r, ix in zip(rs, idx):
                cp = pltpu.make_async_copy(ys_hbm.at[ix],
                                           buf_ref.at[sl, pl.ds(SUBLANES * r, SUBLANES), :],
                                           sem_ref.at[sl])
                cp.start() if start else cp.wait()

    @pl.when(i == 0)
    def _():
        gather(dst_ref, 0, True)

    @pl.when(i + 1 < n)
    def _():
        gather(dstn_ref, 1 - slot, True)

    gather(dst_ref, slot, False)
    g = rg_ref[...]
    cur = buf_ref.at[slot]
    y = (_load_tiled(cur, 0, tc) * g[:, 0:1]
         + _load_tiled(cur, SUBLANES * tc, tc) * g[:, 1:2])
    o_ref[...] = _rms(x1_ref[...] + y, fw_ref[...])


def _combine(x1, rg, ys3, d3, fw):
    n, d = x1.shape
    nt = d3.shape[0]
    tc = n // nt
    return pl.pallas_call(
        _combine_kernel, grid=(nt,), name="combine",
        in_specs=[
            pl.BlockSpec((1, 1, TOP_K * tc), lambda i: (i, 0, 0), memory_space=pltpu.SMEM),
            pl.BlockSpec((1, 1, TOP_K * tc), lambda i: (jnp.minimum(i + 1, nt - 1), 0, 0),
                         memory_space=pltpu.SMEM),
            pl.BlockSpec((tc, d), lambda i: (i, 0)),
            pl.BlockSpec((tc, LANES), lambda i: (i, 0)),
            pl.BlockSpec((1, d), lambda i: (0, 0)),
            pl.BlockSpec(memory_space=pl.ANY),
        ],
        out_specs=pl.BlockSpec((tc, d), lambda i: (i, 0)),
        out_shape=jax.ShapeDtypeStruct((n, d), jnp.float32),
        scratch_shapes=[pltpu.VMEM((2, TOP_K * tc * SUBLANES, LANES), jnp.float32),
                        pltpu.SemaphoreType.DMA((2,))],
        compiler_params=pltpu.CompilerParams(
            dimension_semantics=("arbitrary",), vmem_limit_bytes=VMEM_LIMIT_BYTES),
    )(d3, d3, x1, rg, fw, ys3)


def _rope_tables(positions):
    half = ROT_DIM // 2
    inv_freq = jnp.power(ROPE_THETA, -jnp.arange(half, dtype=jnp.float32) * 2.0 / ROT_DIM)
    ang = positions.astype(jnp.float32)[..., None] * inv_freq
    cs = jnp.concatenate([jnp.cos(ang), jnp.sin(ang)], axis=-1)
    j = jnp.arange(LANES) % HEAD_DIM
    f = jnp.arange(half)[:, None]
    hit = (j[None, :] < ROT_DIM) & ((j[None, :] % half) == f)
    sign = jnp.where(j < half, -1.0, 1.0)[None, :]
    zero = jnp.zeros((half, LANES), jnp.float32)
    sel = jnp.concatenate([jnp.concatenate([hit.astype(jnp.float32), zero], axis=1),
                           jnp.concatenate([zero, hit * sign], axis=1)], axis=0)
    return cs, jnp.tile(sel, (3, 1)).astype(jnp.bfloat16)


def _work_items(counts, bm, ni):
    end = jnp.cumsum(counts)
    start = end - counts
    fb = start // bm
    nblk = jnp.where(counts > 0, (end - 1) // bm - fb + 1, 0)
    iend = jnp.cumsum(nblk)
    nitems = iend[-1]
    i = jnp.minimum(jnp.arange(ni, dtype=jnp.int32), nitems - 1)
    e = jnp.sum((iend[None, :] <= i[:, None]).astype(jnp.int32), axis=1)
    e = jnp.minimum(e, N_EXPERTS - 1)
    blk = fb[e] + i - (iend[e] - nblk[e])
    lo = jnp.maximum(start[e], blk * bm) - blk * bm
    hi = jnp.minimum(end[e], (blk + 1) * bm) - blk * bm
    one = jnp.ones((1,), jnp.int32)
    first = jnp.concatenate([one, (blk[1:] != blk[:-1]).astype(jnp.int32)])
    newexp = jnp.concatenate([one, (e[1:] != e[:-1]).astype(jnp.int32)])
    as32 = lambda a: a.astype(jnp.int32)
    return start, (as32(blk), as32(e), as32(lo), as32(hi), first, newexp,
                   as32(nitems).reshape(1))


def _layer(x, cs, sel, l, attn_norm_w, w_in, attn_sinks, conv_dw_w, conv_dw_b, conv_ln_w,
           conv_ln_b, attn_out_norm_w, conv_out_norm_w, w_out, ffn_norm_w, router_group_w,
           router_group_b, router_expert_w, router_expert_b, w_gate, w_up, w_down, out_norm_w):
    bsz, seq, d = x.shape
    n = bsz * seq
    assert d == SUBLANES * LANES, "token-tiled rows assume a 1024-wide model"
    bf = jnp.bfloat16
    row = lambda a: a.reshape(1, -1)
    wr = jnp.zeros((d, LANES), jnp.float32)
    wr = wr.at[:, :N_GROUPS].set(router_group_w[l]).at[:, N_GROUPS:N_GROUPS + N_EXPERTS].set(
        router_expert_w[l])
    wr_hi = wr.astype(bf)
    wr_lo = (wr - wr_hi.astype(jnp.float32)).astype(bf)
    br = jnp.zeros((1, LANES), jnp.float32)
    br = br.at[0, :N_GROUPS].set(router_group_b[l]).at[0, N_GROUPS:N_GROUPS + N_EXPERTS].set(
        router_expert_b[l])
    ts = min(SEQ_TILE, seq)
    params = dict(
        sinks=attn_sinks[l], sel=sel, anw=row(attn_norm_w[l]), win=w_in[l].astype(bf),
        dww=conv_dw_w[l], dwb=row(conv_dw_b[l]), lnw=row(conv_ln_w[l]), lnb=row(conv_ln_b[l]),
        aonw=row(attn_out_norm_w[l]), conw=row(conv_out_norm_w[l]), wout=w_out[l].astype(bf),
        fnw=row(ffn_norm_w[l]), wr=jnp.stack([wr_hi, wr_lo]), br=br,
        tri=jnp.tri(ts, k=-1, dtype=bf))
    x1, hf2, ri, rg, cnt = _mix(x, cs, params)

    bm = min(EXPERT_BLOCK, n * TOP_K)
    ni = (n * TOP_K) // bm + N_EXPERTS - 1
    counts = cnt[0, :N_EXPERTS].astype(jnp.int32)
    start, items = _work_items(counts, bm, ni)
    nt = n // ts
    hit = ri[:, 0:TOP_K, :, None] == jnp.arange(N_EXPERTS, dtype=jnp.int32)
    seg = jnp.sum(jnp.where(hit, start.astype(jnp.int32), 0), axis=-1)
    d3 = (seg + ri[:, TOP_K:2 * TOP_K, :]).reshape(nt, 1, TOP_K * ts)

    xs3 = _dispatch(hf2.reshape(n, SUBLANES, LANES), d3)
    ys2 = _experts(xs3.reshape(TOP_K * n * SUBLANES, LANES), items,
                   w_gate[l], w_up[l], w_down[l], bm)
    out = _combine(x1.reshape(n, d), rg.reshape(n, LANES),
                   ys2.reshape(TOP_K * n, SUBLANES, LANES), d3, row(out_norm_w))
    return out.reshape(bsz, seq, d)


def kernel(x, positions, attn_norm_w, w_in, attn_sinks, conv_dw_w, conv_dw_b, conv_ln_w, conv_ln_b, attn_out_norm_w, conv_out_norm_w, w_out, ffn_norm_w, router_group_w, router_group_b, router_expert_w, router_expert_b, w_gate, w_up, w_down, final_norm_w):
    depth = attn_norm_w.shape[0]
    assert depth == 1, "the combine kernel applies the final norm, so exactly one layer is fused"
    cs, sel = _rope_tables(positions)
    return _layer(x, cs, sel, 0, attn_norm_w, w_in, attn_sinks, conv_dw_w, conv_dw_b,
                  conv_ln_w, conv_ln_b, attn_out_norm_w, conv_out_norm_w, w_out, ffn_norm_w,
                  router_group_w, router_group_b, router_expert_w, router_expert_b,
                  w_gate, w_up, w_down, final_norm_w)
```

```python
import jax
import jax.numpy as jnp
from jax import lax
from jax.experimental import pallas as pl
from jax.experimental.pallas import tpu as pltpu

HEAD_DIM = 64
N_Q_HEADS = 8
N_KV_HEADS = 2
ATTN_WIDTH = N_Q_HEADS * HEAD_DIM
KV_WIDTH = N_KV_HEADS * HEAD_DIM
CONV_WIDTH = 512
CONV_KERNEL = 31
WINDOW = 128
ROPE_THETA = 500000.0
ROT_DIM = HEAD_DIM // 4
N_GROUPS = 4
EXPERTS_PER_GROUP = 8
N_EXPERTS = N_GROUPS * EXPERTS_PER_GROUP
TOP_K = 2
EPS = 1e-5

LANES = 128
SUBLANES = 8
VMEM_LIMIT_BYTES = 56 * 1024 * 1024

SEQ_TILE = 512
CONV_HALO = 32
CONV_CHUNK = 64
EXPERT_BLOCK = 512
DISPATCH_SLOTS = 3
INDEX_BATCH = 16
DMA_QUEUES = 2
NEG_BIG = -1e30

Q_END = ATTN_WIDTH
K_END = Q_END + KV_WIDTH
V_END = K_END + KV_WIDTH
CA_END = V_END + CONV_WIDTH
IN_COLS = CA_END + CONV_WIDTH


def _rms(x, w):
    return x * lax.rsqrt(jnp.mean(x * x, axis=-1, keepdims=True) + EPS) * w


def _bdot(a, b):
    return jnp.dot(a.astype(jnp.bfloat16), b.astype(jnp.bfloat16),
                   preferred_element_type=jnp.float32)


def _bdot_t(a, b):
    return lax.dot_general(a.astype(jnp.bfloat16), b.astype(jnp.bfloat16),
                           (((1,), (1,)), ((), ())), preferred_element_type=jnp.float32)


def _load_tiled(ref, base, rows):
    d = SUBLANES * LANES
    return jnp.concatenate(
        [ref[pl.ds(base + j, rows, stride=SUBLANES), :] for j in range(d // LANES)], axis=1)


def _store_tiled(ref, val):
    for j in range(val.shape[1] // LANES):
        ref[pl.ds(j, val.shape[0], stride=SUBLANES), :] = val[:, j * LANES:(j + 1) * LANES]


def _mix_kernel(sinks_ref,
                x_ref, cs_ref, sel_ref, anw_ref, win_ref, dww_ref, dwb_ref, lnw_ref, lnb_ref,
                aonw_ref, conw_ref, wout_ref, fnw_ref, wr_ref, br_ref, tri_ref,
                x1_ref, hf_ref, ri_ref, rg_ref, cnt_ref,
                kd_ref, vd_ref, ubuf_ref, ush_ref, attn_ref, run_ref):
    b = pl.program_id(0)
    s = pl.program_id(1)
    ts = x_ref.shape[1]
    nblk = ts // WINDOW
    lane = lax.broadcasted_iota(jnp.int32, (ts, LANES), 1)

    @pl.when(s == 0)
    def _():
        kd_ref[:, 0:WINDOW, :] = jnp.zeros((4, WINDOW, LANES), jnp.bfloat16)
        vd_ref[:, 0:WINDOW, :] = jnp.zeros((4, WINDOW, LANES), jnp.bfloat16)
        ubuf_ref[0:CONV_HALO, :] = jnp.zeros((CONV_HALO, CONV_WIDTH), jnp.float32)

    @pl.when((s == 0) & (b == 0))
    def _():
        run_ref[...] = jnp.zeros_like(run_ref)

    x = x_ref[0]
    hn = _rms(x, anw_ref[...]).astype(jnp.bfloat16)
    proj_c = _bdot(hn, win_ref[:, V_END:IN_COLS])

    u = proj_c[:, 0:CONV_WIDTH] * jax.nn.sigmoid(proj_c[:, CONV_WIDTH:2 * CONV_WIDTH])
    ubuf_ref[CONV_HALO:, :] = u
    span = ush_ref.shape[1]
    for sh in range(1, SUBLANES):
        ush_ref[sh - 1] = ubuf_ref[sh:sh + span, :]
    dwb = dwb_ref[...]
    off0 = CONV_HALO - (CONV_KERNEL - 1)
    convs = []
    for c in range(ts // CONV_CHUNK):
        acc = jnp.broadcast_to(dwb, (CONV_CHUNK, CONV_WIDTH))
        for t in range(CONV_KERNEL):
            a, sh = divmod(off0 + t, SUBLANES)
            r0 = c * CONV_CHUNK + a * SUBLANES
            win = (ubuf_ref[r0:r0 + CONV_CHUNK, :] if sh == 0
                   else ush_ref[sh - 1, r0:r0 + CONV_CHUNK, :])
            acc = acc + win * dww_ref[t:t + 1, :]
        convs.append(acc)
    conv = jnp.concatenate(convs, axis=0)
    ubuf_ref[0:CONV_HALO, :] = ubuf_ref[ts:ts + CONV_HALO, :]
    mu = jnp.mean(conv, axis=-1, keepdims=True)
    xc = conv - mu
    y = xc * lax.rsqrt(jnp.mean(xc * xc, axis=-1, keepdims=True) + EPS) * lnw_ref[...] + lnb_ref[...]
    conv = y * jax.nn.sigmoid(y)
    mixed_c = _rms(conv, conw_ref[...]).astype(jnp.bfloat16)

    proj_kv = _bdot(hn, win_ref[:, Q_END:V_END])
    proj_q = _bdot(hn, win_ref[:, 0:Q_END])

    cs = cs_ref[0]
    cs_hi = cs.astype(jnp.bfloat16)
    cs_r = cs - cs_hi.astype(jnp.float32)
    cs_mid = cs_r.astype(jnp.bfloat16)
    cs_lo = (cs_r - cs_mid.astype(jnp.float32)).astype(jnp.bfloat16)
    tab = jnp.dot(jnp.concatenate([cs_hi, cs_mid, cs_lo], axis=1), sel_ref[...],
                  preferred_element_type=jnp.float32)
    rotary = (lane % HEAD_DIM) < ROT_DIM
    cosv = jnp.where(rotary, tab[:, 0:LANES], 1.0)
    sinv = tab[:, LANES:2 * LANES]
    first_half = (lane % HEAD_DIM) < (ROT_DIM // 2)

    def rope(t):
        partner = jnp.where(first_half, pltpu.roll(t, LANES - ROT_DIM // 2, 1),
                            pltpu.roll(t, ROT_DIM // 2, 1))
        return t * cosv + partner * sinv

    k = rope(proj_kv[:, 0:KV_WIDTH])
    v = proj_kv[:, KV_WIDTH:2 * KV_WIDTH]
    low = lane < HEAD_DIM
    for src, dst in ((k, kd_ref), (v, vd_ref)):
        rolled = pltpu.roll(src, HEAD_DIM, 1)
        zero = jnp.zeros_like(src)
        dst[0, WINDOW:, :] = jnp.where(low, src, zero).astype(jnp.bfloat16)
        dst[1, WINDOW:, :] = jnp.where(low, zero, rolled).astype(jnp.bfloat16)
        dst[2, WINDOW:, :] = jnp.where(low, rolled, zero).astype(jnp.bfloat16)
        dst[3, WINDOW:, :] = jnp.where(low, zero, src).astype(jnp.bfloat16)

    qi = lax.broadcasted_iota(jnp.int32, (WINDOW, 2 * WINDOW), 0)
    kj = lax.broadcasted_iota(jnp.int32, (WINDOW, 2 * WINDOW), 1)
    dist = qi + WINDOW - kj
    band = (dist >= 0) & (dist < WINDOW)
    first_lo = jnp.where(s == 0, WINDOW, 0)
    lane_q = lax.broadcasted_iota(jnp.int32, (WINDOW, LANES), 1)
    for p in range(N_Q_HEADS // 2):
        g = p // 2
        qs = rope(proj_q[:, p * LANES:(p + 1) * LANES]) * (HEAD_DIM ** -0.5)
        for j in range(nblk):
            rows = slice(j * WINDOW, (j + 1) * WINDOW)
            keys = slice(j * WINDOW, (j + 2) * WINDOW)
            valid = (band & (kj >= first_lo)) if j == 0 else band
            qb = qs[rows]
            out = None
            inv = []
            for half in range(2):
                sink = sinks_ref[2 * p + half]
                sc = _bdot_t(qb, kd_ref[2 * g + half, keys, :])
                sc = jnp.where(valid, sc, NEG_BIG)
                m = jnp.maximum(jnp.max(sc, axis=-1, keepdims=True), sink)
                e = jnp.exp(sc - m)
                inv.append(1.0 / (jnp.sum(e, axis=-1, keepdims=True) + jnp.exp(sink - m)))
                o = _bdot(e, vd_ref[2 * g + half, keys, :])
                out = o if out is None else out + o
            attn_ref[rows, p * LANES:(p + 1) * LANES] = out * jnp.where(
                lane_q < HEAD_DIM, inv[0], inv[1])
    for c in range(4):
        kd_ref[c, 0:WINDOW, :] = kd_ref[c, ts:ts + WINDOW, :]
        vd_ref[c, 0:WINDOW, :] = vd_ref[c, ts:ts + WINDOW, :]

    mixed_a = _rms(attn_ref[...], aonw_ref[...])
    x1 = x + _bdot(mixed_a, wout_ref[0:ATTN_WIDTH, :]) + _bdot(mixed_c, wout_ref[ATTN_WIDTH:, :])
    x1_ref[0] = x1
    hf = _rms(x1, fnw_ref[...])
    _store_tiled(hf_ref, hf)

    hf_hi = hf.astype(jnp.bfloat16)
    hf_lo = (hf - hf_hi.astype(jnp.float32)).astype(jnp.bfloat16)
    logits = (jnp.dot(hf_hi, wr_ref[0], preferred_element_type=jnp.float32)
              + jnp.dot(hf_lo, wr_ref[0], preferred_element_type=jnp.float32)
              + jnp.dot(hf_hi, wr_ref[1], preferred_element_type=jnp.float32)) + br_ref[...]

    lanef = lane.astype(jnp.float32)
    big = float(LANES)
    gmask = lane < N_GROUPS
    gmax = jnp.max(jnp.where(gmask, logits, -jnp.inf), axis=-1, keepdims=True)
    gidx = jnp.min(jnp.where(gmask & (logits == gmax), lanef, big), axis=-1, keepdims=True)
    gsum = jnp.sum(jnp.where(gmask, jnp.exp(logits - gmax), 0.0), axis=-1, keepdims=True)
    g_p = 1.0 / gsum
    elo = N_GROUPS + EXPERTS_PER_GROUP * gidx
    emask = (lanef >= elo) & (lanef < elo + EXPERTS_PER_GROUP)
    em1 = jnp.max(jnp.where(emask, logits, -jnp.inf), axis=-1, keepdims=True)
    i1 = jnp.min(jnp.where(emask & (logits == em1), lanef, big), axis=-1, keepdims=True)
    mask2 = emask & (lanef != i1)
    em2 = jnp.max(jnp.where(mask2, logits, -jnp.inf), axis=-1, keepdims=True)
    i2 = jnp.min(jnp.where(mask2 & (logits == em2), lanef, big), axis=-1, keepdims=True)
    esum = jnp.sum(jnp.where(emask, jnp.exp(logits - em1), 0.0), axis=-1, keepdims=True)
    p1 = 1.0 / esum
    p2 = jnp.exp(em2 - em1) / esum
    gate1 = g_p * p1 / (p1 + p2)
    gate2 = g_p * p2 / (p1 + p2)
    e1 = i1 - N_GROUPS
    e2 = i2 - N_GROUPS

    oh1 = lanef == e1
    oh2 = lanef == e2
    onehot = jnp.where(oh1 | oh2, 1.0, 0.0)
    tot = jnp.dot(tri_ref[...], onehot.astype(jnp.bfloat16),
                  preferred_element_type=jnp.float32) + run_ref[0:1, :]
    r1 = jnp.sum(jnp.where(oh1, tot, 0.0), axis=-1, keepdims=True)
    r2 = jnp.sum(jnp.where(oh2, tot, 0.0), axis=-1, keepdims=True)
    run_ref[...] = run_ref[...] + jnp.sum(onehot, axis=0, keepdims=True)
    cnt_ref[...] = run_ref[...]

    ri = jnp.where(lane == 0, e1, jnp.where(lane == 1, e2, jnp.where(lane == 2, r1, r2)))
    ri_ref[0] = ri.T[0:SUBLANES, :].astype(jnp.int32)
    rg_ref[0] = jnp.where(lane == 0, gate1, gate2)


def _mix(x, cs, p):
    bsz, seq, d = x.shape
    ts = min(SEQ_TILE, seq)
    nst = seq // ts
    grid = (bsz, nst)
    tile = lambda last: pl.BlockSpec((1, ts, last), lambda b, s, *_: (b, s, 0))
    full = lambda a: pl.BlockSpec(a.shape, lambda b, s, *_: (0,) * a.ndim)
    weights = [p["sel"], p["anw"], p["win"], p["dww"], p["dwb"], p["lnw"], p["lnb"], p["aonw"], p["conw"],
               p["wout"], p["fnw"], p["wr"], p["br"], p["tri"]]
    out_shape = [
        jax.ShapeDtypeStruct((bsz, seq, d), jnp.float32),
        jax.ShapeDtypeStruct((bsz * seq * SUBLANES, LANES), jnp.float32),
        jax.ShapeDtypeStruct((bsz * nst, SUBLANES, ts), jnp.int32),
        jax.ShapeDtypeStruct((bsz, seq, LANES), jnp.float32),
        jax.ShapeDtypeStruct((SUBLANES, LANES), jnp.float32),
    ]
    out_specs = [tile(d),
                 pl.BlockSpec((ts * SUBLANES, LANES), lambda b, s, *_: (b * nst + s, 0)),
                 pl.BlockSpec((1, SUBLANES, ts), lambda b, s, *_: (b * nst + s, 0, 0)),
                 tile(LANES),
                 pl.BlockSpec((SUBLANES, LANES), lambda b, s, *_: (0, 0))]
    grid_spec = pltpu.PrefetchScalarGridSpec(
        num_scalar_prefetch=1, grid=grid,
        in_specs=[tile(d), tile(cs.shape[2])] + [full(w) for w in weights],
        out_specs=out_specs,
        scratch_shapes=[
            pltpu.VMEM((4, ts + WINDOW, LANES), jnp.bfloat16),
            pltpu.VMEM((4, ts + WINDOW, LANES), jnp.bfloat16),
            pltpu.VMEM((ts + CONV_HALO, CONV_WIDTH), jnp.float32),
            pltpu.VMEM((SUBLANES - 1, ts + CONV_HALO - SUBLANES, CONV_WIDTH), jnp.float32),
            pltpu.VMEM((ts, ATTN_WIDTH), jnp.float32),
            pltpu.VMEM((SUBLANES, LANES), jnp.float32),
        ])
    return pl.pallas_call(
        _mix_kernel, grid_spec=grid_spec, out_shape=out_shape, name="mix",
        compiler_params=pltpu.CompilerParams(
            dimension_semantics=("arbitrary", "arbitrary"),
            vmem_limit_bytes=VMEM_LIMIT_BYTES),
    )(p["sinks"], x, cs, *weights)


def _dispatch_kernel(dst_ref, hf_hbm, xs_hbm, buf_ref, in_sem, out_sem):
    i = pl.program_id(0)
    n = pl.num_programs(0)
    td = buf_ref.shape[1]
    nslot = buf_ref.shape[0]

    def tile_in(t):
        return pltpu.make_async_copy(hf_hbm.at[pl.ds(t * td, td)], buf_ref.at[t % nslot],
                                     in_sem.at[t % nslot])

    def rows(t, start):
        src = buf_ref.at[t % nslot]
        for q0 in range(0, TOP_K * td, INDEX_BATCH):
            qs = range(q0, q0 + INDEX_BATCH)
            idx = [dst_ref[0, 0, q] if start else 0 for q in qs]
            for q, ix in zip(qs, idx):
                cp = pltpu.make_async_copy(src.at[q % td], xs_hbm.at[ix], out_sem.at[t % nslot])
                cp.start(priority=q % DMA_QUEUES) if start else cp.wait()

    @pl.when(i == 0)
    def _():
        tile_in(0).start()

    @pl.when((i == 0) & (n > 1))
    def _():
        tile_in(1).start()

    tile_in(i).wait()
    rows(i, True)

    @pl.when(i > 0)
    def _():
        rows(i - 1, False)

    @pl.when(i + 2 < n)
    def _():
        tile_in(i + 2).start()

    @pl.when(i == n - 1)
    def _():
        rows(i, False)


def _dispatch(hf3, d3):
    n = hf3.shape[0]
    nt = d3.shape[0]
    td = n // nt
    return pl.pallas_call(
        _dispatch_kernel, grid=(nt,), name="dispatch",
        in_specs=[pl.BlockSpec((1, 1, TOP_K * td), lambda i: (i, 0, 0), memory_space=pltpu.SMEM),
                  pl.BlockSpec(memory_space=pl.ANY)],
        out_specs=pl.BlockSpec(memory_space=pl.ANY),
        out_shape=jax.ShapeDtypeStruct((TOP_K * n, SUBLANES, LANES), jnp.float32),
        scratch_shapes=[pltpu.VMEM((DISPATCH_SLOTS, td, SUBLANES, LANES), jnp.float32),
                        pltpu.SemaphoreType.DMA((DISPATCH_SLOTS,)),
                        pltpu.SemaphoreType.DMA((DISPATCH_SLOTS,))],
        compiler_params=pltpu.CompilerParams(dimension_semantics=("arbitrary",)),
    )(d3, hf3)


def _expert_kernel(blk_ref, exp_ref, lo_ref, hi_ref, first_ref, newexp_ref, nitems_ref,
                   xs_ref, wg_ref, wu_ref, wd_ref, ys_ref, wgb_ref, wub_ref, wdb_ref):
    i = pl.program_id(0)
    bm = xs_ref.shape[0] // SUBLANES

    @pl.when((i < nitems_ref[0]) & (newexp_ref[i] == 1))
    def _():
        wgb_ref[...] = wg_ref[0].astype(jnp.bfloat16)
        wub_ref[...] = wu_ref[0].astype(jnp.bfloat16)
        wdb_ref[...] = wd_ref[0].astype(jnp.bfloat16)

    @pl.when(i < nitems_ref[0])
    def _():
        xb = _load_tiled(xs_ref, 0, bm).astype(jnp.bfloat16)
        hg = jnp.dot(xb, wgb_ref[...], preferred_element_type=jnp.float32)
        hu = jnp.dot(xb, wub_ref[...], preferred_element_type=jnp.float32)
        h = hg * jax.nn.sigmoid(hg) * hu
        y = jnp.dot(h.astype(jnp.bfloat16), wdb_ref[...], preferred_element_type=jnp.float32)
        row = lax.broadcasted_iota(jnp.int32, (bm, 1), 0)
        y = jnp.where((row >= lo_ref[i]) & (row < hi_ref[i]), y, 0.0)

        @pl.when(first_ref[i] == 1)
        def _():
            _store_tiled(ys_ref, y)

        @pl.when(first_ref[i] == 0)
        def _():
            _store_tiled(ys_ref, _load_tiled(ys_ref, 0, bm) + y)


def _experts(xs2, items, wg, wu, wd, bm):
    rows = xs2.shape[0] // SUBLANES
    ni = items[0].shape[0]
    d, de = wg.shape[1], wg.shape[2]
    blk = lambda i, b, e, *_: (b[i], 0)
    wsp = lambda shape: pl.BlockSpec(shape, lambda i, b, e, *_: (e[i], 0, 0))
    grid_spec = pltpu.PrefetchScalarGridSpec(
        num_scalar_prefetch=len(items), grid=(ni,),
        in_specs=[pl.BlockSpec((bm * SUBLANES, LANES), blk),
                  wsp((1, d, de)), wsp((1, d, de)), wsp((1, de, d))],
        out_specs=pl.BlockSpec((bm * SUBLANES, LANES), blk),
        scratch_shapes=[pltpu.VMEM((d, de), jnp.bfloat16), pltpu.VMEM((d, de), jnp.bfloat16),
                        pltpu.VMEM((de, d), jnp.bfloat16)])
    return pl.pallas_call(
        _expert_kernel, grid_spec=grid_spec, name="experts",
        out_shape=jax.ShapeDtypeStruct((rows * SUBLANES, LANES), jnp.float32),
        compiler_params=pltpu.CompilerParams(
            dimension_semantics=("arbitrary",), vmem_limit_bytes=VMEM_LIMIT_BYTES),
    )(*items, xs2, wg, wu, wd)


def _combine_kernel(dst_ref, dstn_ref, x1_ref, rg_ref, fw_ref, ys_hbm, o_ref, buf_ref, sem_ref):
    i = pl.program_id(0)
    n = pl.num_programs(0)
    tc = x1_ref.shape[0]
    slot = i % 2

    def gather(idx_ref, sl, start):
        for r0 in range(0, TOP_K * tc, INDEX_BATCH):
            rs = range(r0, r0 + INDEX_BATCH)
            idx = [idx_ref[0, 0, r] if start else 0 for r in rs]
            for r, ix in zip(rs, idx):
                cp = pltpu.make_async_copy(ys_hbm.at[ix],
                                           buf_ref.at[sl, pl.ds(SUBLANES * r, SUBLANES), :],
                                           sem_ref.at[sl])
                cp.start(priority=r % DMA_QUEUES) if start else cp.wait()

    @pl.when(i == 0)
    def _():
        gather(dst_ref, 0, True)

    @pl.when(i + 1 < n)
    def _():
        gather(dstn_ref, 1 - slot, True)

    gather(dst_ref, slot, False)
    g = rg_ref[...]
    cur = buf_ref.at[slot]
    y = (_load_tiled(cur, 0, tc) * g[:, 0:1]
         + _load_tiled(cur, SUBLANES * tc, tc) * g[:, 1:2])
    o_ref[...] = _rms(x1_ref[...] + y, fw_ref[...])


def _combine(x1, rg, ys3, d3, fw):
    n, d = x1.shape
    nt = d3.shape[0]
    tc = n // nt
    return pl.pallas_call(
        _combine_kernel, grid=(nt,), name="combine",
        in_specs=[
            pl.BlockSpec((1, 1, TOP_K * tc), lambda i: (i, 0, 0), memory_space=pltpu.SMEM),
            pl.BlockSpec((1, 1, TOP_K * tc), lambda i: (jnp.minimum(i + 1, nt - 1), 0, 0),
                         memory_space=pltpu.SMEM),
            pl.BlockSpec((tc, d), lambda i: (i, 0)),
            pl.BlockSpec((tc, LANES), lambda i: (i, 0)),
            pl.BlockSpec((1, d), lambda i: (0, 0)),
            pl.BlockSpec(memory_space=pl.ANY),
        ],
        out_specs=pl.BlockSpec((tc, d), lambda i: (i, 0)),
        out_shape=jax.ShapeDtypeStruct((n, d), jnp.float32),
        scratch_shapes=[pltpu.VMEM((2, TOP_K * tc * SUBLANES, LANES), jnp.float32),
                        pltpu.SemaphoreType.DMA((2,))],
        compiler_params=pltpu.CompilerParams(
            dimension_semantics=("arbitrary",), vmem_limit_bytes=VMEM_LIMIT_BYTES),
    )(d3, d3, x1, rg, fw, ys3)


def _rope_tables(positions):
    half = ROT_DIM // 2
    inv_freq = jnp.power(ROPE_THETA, -jnp.arange(half, dtype=jnp.float32) * 2.0 / ROT_DIM)
    ang = positions.astype(jnp.float32)[..., None] * inv_freq
    cs = jnp.concatenate([jnp.cos(ang), jnp.sin(ang)], axis=-1)
    j = jnp.arange(LANES) % HEAD_DIM
    f = jnp.arange(half)[:, None]
    hit = (j[None, :] < ROT_DIM) & ((j[None, :] % half) == f)
    sign = jnp.where(j < half, -1.0, 1.0)[None, :]
    zero = jnp.zeros((half, LANES), jnp.float32)
    sel = jnp.concatenate([jnp.concatenate([hit.astype(jnp.float32), zero], axis=1),
                           jnp.concatenate([zero, hit * sign], axis=1)], axis=0)
    return cs, jnp.tile(sel, (3, 1)).astype(jnp.bfloat16)


def _work_items(counts, bm, ni):
    end = jnp.cumsum(counts)
    start = end - counts
    fb = start // bm
    nblk = jnp.where(counts > 0, (end - 1) // bm - fb + 1, 0)
    iend = jnp.cumsum(nblk)
    nitems = iend[-1]
    i = jnp.minimum(jnp.arange(ni, dtype=jnp.int32), nitems - 1)
    e = jnp.sum((iend[None, :] <= i[:, None]).astype(jnp.int32), axis=1)
    e = jnp.minimum(e, N_EXPERTS - 1)
    blk = fb[e] + i - (iend[e] - nblk[e])
    lo = jnp.maximum(start[e], blk * bm) - blk * bm
    hi = jnp.minimum(end[e], (blk + 1) * bm) - blk * bm
    one = jnp.ones((1,), jnp.int32)
    first = jnp.concatenate([one, (blk[1:] != blk[:-1]).astype(jnp.int32)])
    newexp = jnp.concatenate([one, (e[1:] != e[:-1]).astype(jnp.int32)])
    as32 = lambda a: a.astype(jnp.int32)
    return start, (as32(blk), as32(e), as32(lo), as32(hi), first, newexp,
                   as32(nitems).reshape(1))


def _layer(x, cs, sel, l, attn_norm_w, w_in, attn_sinks, conv_dw_w, conv_dw_b, conv_ln_w,
           conv_ln_b, attn_out_norm_w, conv_out_norm_w, w_out, ffn_norm_w, router_group_w,
           router_group_b, router_expert_w, router_expert_b, w_gate, w_up, w_down, out_norm_w):
    bsz, seq, d = x.shape
    n = bsz * seq
    assert d == SUBLANES * LANES, "token-tiled rows assume a 1024-wide model"
    bf = jnp.bfloat16
    row = lambda a: a.reshape(1, -1)
    wr = jnp.zeros((d, LANES), jnp.float32)
    wr = wr.at[:, :N_GROUPS].set(router_group_w[l]).at[:, N_GROUPS:N_GROUPS + N_EXPERTS].set(
        router_expert_w[l])
    wr_hi = wr.astype(bf)
    wr_lo = (wr - wr_hi.astype(jnp.float32)).astype(bf)
    br = jnp.zeros((1, LANES), jnp.float32)
    br = br.at[0, :N_GROUPS].set(router_group_b[l]).at[0, N_GROUPS:N_GROUPS + N_EXPERTS].set(
        router_expert_b[l])
    ts = min(SEQ_TILE, seq)
    params = dict(
        sinks=attn_sinks[l], sel=sel, anw=row(attn_norm_w[l]), win=w_in[l].astype(bf),
        dww=conv_dw_w[l], dwb=row(conv_dw_b[l]), lnw=row(conv_ln_w[l]), lnb=row(conv_ln_b[l]),
        aonw=row(attn_out_norm_w[l]), conw=row(conv_out_norm_w[l]), wout=w_out[l].astype(bf),
        fnw=row(ffn_norm_w[l]), wr=jnp.stack([wr_hi, wr_lo]), br=br,
        tri=jnp.tri(ts, k=-1, dtype=bf))
    x1, hf2, ri, rg, cnt = _mix(x, cs, params)

    bm = min(EXPERT_BLOCK, n * TOP_K)
    ni = (n * TOP_K) // bm + N_EXPERTS - 1
    counts = cnt[0, :N_EXPERTS].astype(jnp.int32)
    start, items = _work_items(counts, bm, ni)
    nt = n // ts
    hit = ri[:, 0:TOP_K, :, None] == jnp.arange(N_EXPERTS, dtype=jnp.int32)
    seg = jnp.sum(jnp.where(hit, start.astype(jnp.int32), 0), axis=-1)
    d3 = (seg + ri[:, TOP_K:2 * TOP_K, :]).reshape(nt, 1, TOP_K * ts)

    xs3 = _dispatch(hf2.reshape(n, SUBLANES, LANES), d3)
    ys2 = _experts(xs3.reshape(TOP_K * n * SUBLANES, LANES), items,
                   w_gate[l], w_up[l], w_down[l], bm)
    out = _combine(x1.reshape(n, d), rg.reshape(n, LANES),
                   ys2.reshape(TOP_K * n, SUBLANES, LANES), d3, row(out_norm_w))
    return out.reshape(bsz, seq, d)


def kernel(x, positions, attn_norm_w, w_in, attn_sinks, conv_dw_w, conv_dw_b, conv_ln_w, conv_ln_b, attn_out_norm_w, conv_out_norm_w, w_out, ffn_norm_w, router_group_w, router_group_b, router_expert_w, router_expert_b, w_gate, w_up, w_down, final_norm_w):
    depth = attn_norm_w.shape[0]
    assert depth == 1, "the combine kernel applies the final norm, so exactly one layer is fused"
    cs, sel = _rope_tables(positions)
    return _layer(x, cs, sel, 0, attn_norm_w, w_in, attn_sinks, conv_dw_w, conv_dw_b,
                  conv_ln_w, conv_ln_b, attn_out_norm_w, conv_out_norm_w, w_out, ffn_norm_w,
                  router_group_w, router_group_b, router_expert_w, router_expert_b,
                  w_gate, w_up, w_down, final_norm_w)
```

```python
import jax
import jax.numpy as jnp
from jax import lax
from jax.experimental import pallas as pl
from jax.experimental.pallas import tpu as pltpu

HEAD_DIM = 64
N_Q_HEADS = 8
N_KV_HEADS = 2
ATTN_WIDTH = N_Q_HEADS * HEAD_DIM
KV_WIDTH = N_KV_HEADS * HEAD_DIM
CONV_WIDTH = 512
CONV_KERNEL = 31
WINDOW = 128
ROPE_THETA = 500000.0
ROT_DIM = HEAD_DIM // 4
N_GROUPS = 4
EXPERTS_PER_GROUP = 8
N_EXPERTS = N_GROUPS * EXPERTS_PER_GROUP
TOP_K = 2
EPS = 1e-5

LANES = 128
SUBLANES = 8
VMEM_LIMIT_BYTES = 56 * 1024 * 1024

SEQ_TILE = 512
CONV_HALO = 32
CONV_CHUNK = 64
EXPERT_BLOCK = 512
DISPATCH_SLOTS = 3
INDEX_BATCH = 16
DMA_QUEUES = 2
NEG_BIG = -1e30

Q_END = ATTN_WIDTH
K_END = Q_END + KV_WIDTH
V_END = K_END + KV_WIDTH
CA_END = V_END + CONV_WIDTH
IN_COLS = CA_END + CONV_WIDTH


def _rms(x, w):
    return x * lax.rsqrt(jnp.mean(x * x, axis=-1, keepdims=True) + EPS) * w


def _bdot(a, b):
    return jnp.dot(a.astype(jnp.bfloat16), b.astype(jnp.bfloat16),
                   preferred_element_type=jnp.float32)


def _bdot_t(a, b):
    return lax.dot_general(a.astype(jnp.bfloat16), b.astype(jnp.bfloat16),
                           (((1,), (1,)), ((), ())), preferred_element_type=jnp.float32)


def _load_tiled(ref, base, rows):
    d = SUBLANES * LANES
    return jnp.concatenate(
        [ref[pl.ds(base + j, rows, stride=SUBLANES), :] for j in range(d // LANES)], axis=1)


def _store_tiled(ref, val):
    for j in range(val.shape[1] // LANES):
        ref[pl.ds(j, val.shape[0], stride=SUBLANES), :] = val[:, j * LANES:(j + 1) * LANES]


def _mix_kernel(sinks_ref,
                x_ref, cs_ref, sel_ref, anw_ref, win_ref, dww_ref, dwb_ref, lnw_ref, lnb_ref,
                aonw_ref, conw_ref, wout_ref, fnw_ref, wr_ref, br_ref, tri_ref,
                x1_ref, hf_ref, ri_ref, rg_ref, cnt_ref,
                kd_ref, vd_ref, ubuf_ref, ush_ref, attn_ref, run_ref):
    b = pl.program_id(0)
    s = pl.program_id(1)
    ts = x_ref.shape[1]
    nblk = ts // WINDOW
    lane = lax.broadcasted_iota(jnp.int32, (ts, LANES), 1)

    @pl.when(s == 0)
    def _():
        kd_ref[:, 0:WINDOW, :] = jnp.zeros((4, WINDOW, LANES), jnp.bfloat16)
        vd_ref[:, 0:WINDOW, :] = jnp.zeros((4, WINDOW, LANES), jnp.bfloat16)
        ubuf_ref[0:CONV_HALO, :] = jnp.zeros((CONV_HALO, CONV_WIDTH), jnp.float32)

    @pl.when((s == 0) & (b == 0))
    def _():
        run_ref[...] = jnp.zeros_like(run_ref)

    x = x_ref[0]
    hn = _rms(x, anw_ref[...]).astype(jnp.bfloat16)
    proj_c = _bdot(hn, win_ref[:, V_END:IN_COLS])

    u = proj_c[:, 0:CONV_WIDTH] * jax.nn.sigmoid(proj_c[:, CONV_WIDTH:2 * CONV_WIDTH])
    ubuf_ref[CONV_HALO:, :] = u
    span = ush_ref.shape[1]
    for sh in range(1, SUBLANES):
        ush_ref[sh - 1] = ubuf_ref[sh:sh + span, :]
    dwb = dwb_ref[...]
    off0 = CONV_HALO - (CONV_KERNEL - 1)
    convs = []
    for c in range(ts // CONV_CHUNK):
        acc = jnp.broadcast_to(dwb, (CONV_CHUNK, CONV_WIDTH))
        for t in range(CONV_KERNEL):
            a, sh = divmod(off0 + t, SUBLANES)
            r0 = c * CONV_CHUNK + a * SUBLANES
            win = (ubuf_ref[r0:r0 + CONV_CHUNK, :] if sh == 0
                   else ush_ref[sh - 1, r0:r0 + CONV_CHUNK, :])
            acc = acc + win * dww_ref[t:t + 1, :]
        convs.append(acc)
    conv = jnp.concatenate(convs, axis=0)
    ubuf_ref[0:CONV_HALO, :] = ubuf_ref[ts:ts + CONV_HALO, :]
    mu = jnp.mean(conv, axis=-1, keepdims=True)
    xc = conv - mu
    y = xc * lax.rsqrt(jnp.mean(xc * xc, axis=-1, keepdims=True) + EPS) * lnw_ref[...] + lnb_ref[...]
    conv = y * jax.nn.sigmoid(y)
    mixed_c = _rms(conv, conw_ref[...]).astype(jnp.bfloat16)

    proj_kv = _bdot(hn, win_ref[:, Q_END:V_END])
    proj_q = _bdot(hn, win_ref[:, 0:Q_END])

    cs = cs_ref[0]
    cs_hi = cs.astype(jnp.bfloat16)
    cs_r = cs - cs_hi.astype(jnp.float32)
    cs_mid = cs_r.astype(jnp.bfloat16)
    cs_lo = (cs_r - cs_mid.astype(jnp.float32)).astype(jnp.bfloat16)
    tab = jnp.dot(jnp.concatenate([cs_hi, cs_mid, cs_lo], axis=1), sel_ref[...],
                  preferred_element_type=jnp.float32)
    rotary = (lane % HEAD_DIM) < ROT_DIM
    cosv = jnp.where(rotary, tab[:, 0:LANES], 1.0)
    sinv = tab[:, LANES:2 * LANES]
    first_half = (lane % HEAD_DIM) < (ROT_DIM // 2)

    def rope(t):
        partner = jnp.where(first_half, pltpu.roll(t, LANES - ROT_DIM // 2, 1),
                            pltpu.roll(t, ROT_DIM // 2, 1))
        return t * cosv + partner * sinv

    k = rope(proj_kv[:, 0:KV_WIDTH])
    v = proj_kv[:, KV_WIDTH:2 * KV_WIDTH]
    low = lane < HEAD_DIM
    for src, dst in ((k, kd_ref), (v, vd_ref)):
        rolled = pltpu.roll(src, HEAD_DIM, 1)
        zero = jnp.zeros_like(src)
        dst[0, WINDOW:, :] = jnp.where(low, src, zero).astype(jnp.bfloat16)
        dst[1, WINDOW:, :] = jnp.where(low, zero, rolled).astype(jnp.bfloat16)
        dst[2, WINDOW:, :] = jnp.where(low, rolled, zero).astype(jnp.bfloat16)
        dst[3, WINDOW:, :] = jnp.where(low, zero, src).astype(jnp.bfloat16)

    qi = lax.broadcasted_iota(jnp.int32, (WINDOW, 2 * WINDOW), 0)
    kj = lax.broadcasted_iota(jnp.int32, (WINDOW, 2 * WINDOW), 1)
    dist = qi + WINDOW - kj
    band = (dist >= 0) & (dist < WINDOW)
    first_lo = jnp.where(s == 0, WINDOW, 0)
    lane_q = lax.broadcasted_iota(jnp.int32, (WINDOW, LANES), 1)
    for p in range(N_Q_HEADS // 2):
        g = p // 2
        qs = rope(proj_q[:, p * LANES:(p + 1) * LANES]) * (HEAD_DIM ** -0.5)
        for j in range(nblk):
            rows = slice(j * WINDOW, (j + 1) * WINDOW)
            keys = slice(j * WINDOW, (j + 2) * WINDOW)
            valid = (band & (kj >= first_lo)) if j == 0 else band
            qb = qs[rows]
            out = None
            inv = []
            for half in range(2):
                sink = sinks_ref[2 * p + half]
                sc = _bdot_t(qb, kd_ref[2 * g + half, keys, :])
                sc = jnp.where(valid, sc, NEG_BIG)
                m = jnp.maximum(jnp.max(sc, axis=-1, keepdims=True), sink)
                e = jnp.exp(sc - m)
                inv.append(1.0 / (jnp.sum(e, axis=-1, keepdims=True) + jnp.exp(sink - m)))
                o = _bdot(e, vd_ref[2 * g + half, keys, :])
                out = o if out is None else out + o
            attn_ref[rows, p * LANES:(p + 1) * LANES] = out * jnp.where(
                lane_q < HEAD_DIM, inv[0], inv[1])
    for c in range(4):
        kd_ref[c, 0:WINDOW, :] = kd_ref[c, ts:ts + WINDOW, :]
        vd_ref[c, 0:WINDOW, :] = vd_ref[c, ts:ts + WINDOW, :]

    mixed_a = _rms(attn_ref[...], aonw_ref[...])
    x1 = x + _bdot(mixed_a, wout_ref[0:ATTN_WIDTH, :]) + _bdot(mixed_c, wout_ref[ATTN_WIDTH:, :])
    x1_ref[0] = x1
    hf = _rms(x1, fnw_ref[...])
    _store_tiled(hf_ref, hf)

    hf_hi = hf.astype(jnp.bfloat16)
    hf_lo = (hf - hf_hi.astype(jnp.float32)).astype(jnp.bfloat16)
    logits = (jnp.dot(hf_hi, wr_ref[0], preferred_element_type=jnp.float32)
              + jnp.dot(hf_lo, wr_ref[0], preferred_element_type=jnp.float32)
              + jnp.dot(hf_hi, wr_ref[1], preferred_element_type=jnp.float32)) + br_ref[...]

    lanef = lane.astype(jnp.float32)
    big = float(LANES)
    gmask = lane < N_GROUPS
    gmax = jnp.max(jnp.where(gmask, logits, -jnp.inf), axis=-1, keepdims=True)
    gidx = jnp.min(jnp.where(gmask & (logits == gmax), lanef, big), axis=-1, keepdims=True)
    gsum = jnp.sum(jnp.where(gmask, jnp.exp(logits - gmax), 0.0), axis=-1, keepdims=True)
    g_p = 1.0 / gsum
    elo = N_GROUPS + EXPERTS_PER_GROUP * gidx
    emask = (lanef >= elo) & (lanef < elo + EXPERTS_PER_GROUP)
    em1 = jnp.max(jnp.where(emask, logits, -jnp.inf), axis=-1, keepdims=True)
    i1 = jnp.min(jnp.where(emask & (logits == em1), lanef, big), axis=-1, keepdims=True)
    mask2 = emask & (lanef != i1)
    em2 = jnp.max(jnp.where(mask2, logits, -jnp.inf), axis=-1, keepdims=True)
    i2 = jnp.min(jnp.where(mask2 & (logits == em2), lanef, big), axis=-1, keepdims=True)
    esum = jnp.sum(jnp.where(emask, jnp.exp(logits - em1), 0.0), axis=-1, keepdims=True)
    p1 = 1.0 / esum
    p2 = jnp.exp(em2 - em1) / esum
    gate1 = g_p * p1 / (p1 + p2)
    gate2 = g_p * p2 / (p1 + p2)
    e1 = i1 - N_GROUPS
    e2 = i2 - N_GROUPS

    oh1 = lanef == e1
    oh2 = lanef == e2
    onehot = jnp.where(oh1 | oh2, 1.0, 0.0)
    tot = jnp.dot(tri_ref[...], onehot.astype(jnp.bfloat16),
                  preferred_element_type=jnp.float32) + run_ref[0:1, :]
    r1 = jnp.sum(jnp.where(oh1, tot, 0.0), axis=-1, keepdims=True)
    r2 = jnp.sum(jnp.where(oh2, tot, 0.0), axis=-1, keepdims=True)
    run_ref[...] = run_ref[...] + jnp.sum(onehot, axis=0, keepdims=True)
    cnt_ref[...] = run_ref[...]

    ri = jnp.where(lane == 0, e1, jnp.where(lane == 1, e2, jnp.where(lane == 2, r1, r2)))
    ri_ref[0] = ri.T[0:SUBLANES, :].astype(jnp.int32)
    rg_ref[0] = jnp.where(lane == 0, gate1, gate2)


def _mix(x, cs, p):
    bsz, seq, d = x.shape
    ts = min(SEQ_TILE, seq)
    nst = seq // ts
    grid = (bsz, nst)
    tile = lambda last: pl.BlockSpec((1, ts, last), lambda b, s, *_: (b, s, 0))
    full = lambda a: pl.BlockSpec(a.shape, lambda b, s, *_: (0,) * a.ndim)
    weights = [p["sel"], p["anw"], p["win"], p["dww"], p["dwb"], p["lnw"], p["lnb"], p["aonw"], p["conw"],
               p["wout"], p["fnw"], p["wr"], p["br"], p["tri"]]
    out_shape = [
        jax.ShapeDtypeStruct((bsz, seq, d), jnp.float32),
        jax.ShapeDtypeStruct((bsz * seq * SUBLANES, LANES), jnp.float32),
        jax.ShapeDtypeStruct((bsz * nst, SUBLANES, ts), jnp.int32),
        jax.ShapeDtypeStruct((bsz, seq, LANES), jnp.float32),
        jax.ShapeDtypeStruct((SUBLANES, LANES), jnp.float32),
    ]
    out_specs = [tile(d),
                 pl.BlockSpec((ts * SUBLANES, LANES), lambda b, s, *_: (b * nst + s, 0)),
                 pl.BlockSpec((1, SUBLANES, ts), lambda b, s, *_: (b * nst + s, 0, 0)),
                 tile(LANES),
                 pl.BlockSpec((SUBLANES, LANES), lambda b, s, *_: (0, 0))]
    grid_spec = pltpu.PrefetchScalarGridSpec(
        num_scalar_prefetch=1, grid=grid,
        in_specs=[tile(d), tile(cs.shape[2])] + [full(w) for w in weights],
        out_specs=out_specs,
        scratch_shapes=[
            pltpu.VMEM((4, ts + WINDOW, LANES), jnp.bfloat16),
            pltpu.VMEM((4, ts + WINDOW, LANES), jnp.bfloat16),
            pltpu.VMEM((ts + CONV_HALO, CONV_WIDTH), jnp.float32),
            pltpu.VMEM((SUBLANES - 1, ts + CONV_HALO - SUBLANES, CONV_WIDTH), jnp.float32),
            pltpu.VMEM((ts, ATTN_WIDTH), jnp.float32),
            pltpu.VMEM((SUBLANES, LANES), jnp.float32),
        ])
    return pl.pallas_call(
        _mix_kernel, grid_spec=grid_spec, out_shape=out_shape, name="mix",
        compiler_params=pltpu.CompilerParams(
            dimension_semantics=("arbitrary", "arbitrary"),
            vmem_limit_bytes=VMEM_LIMIT_BYTES),
    )(p["sinks"], x, cs, *weights)


def _dispatch_kernel(dst_ref, hf_hbm, xs_hbm, buf_ref, in_sem, out_sem):
    i = pl.program_id(0)
    n = pl.num_programs(0)
    td = buf_ref.shape[1]
    nslot = buf_ref.shape[0]

    def tile_in(t):
        return pltpu.make_async_copy(hf_hbm.at[pl.ds(t * td, td)], buf_ref.at[t % nslot],
                                     in_sem.at[t % nslot])

    def rows(t, start):
        src = buf_ref.at[t % nslot]
        for q0 in range(0, TOP_K * td, INDEX_BATCH):
            qs = range(q0, q0 + INDEX_BATCH)
            idx = [dst_ref[0, 0, q] if start else 0 for q in qs]
            for q, ix in zip(qs, idx):
                cp = pltpu.make_async_copy(src.at[q % td], xs_hbm.at[ix], out_sem.at[t % nslot])
                cp.start(priority=q % DMA_QUEUES) if start else cp.wait()

    @pl.when(i == 0)
    def _():
        tile_in(0).start()

    @pl.when((i == 0) & (n > 1))
    def _():
        tile_in(1).start()

    tile_in(i).wait()
    rows(i, True)

    @pl.when(i > 0)
    def _():
        rows(i - 1, False)

    @pl.when(i + 2 < n)
    def _():
        tile_in(i + 2).start()

    @pl.when(i == n - 1)
    def _():
        rows(i, False)


def _dispatch(hf3, d3):
    n = hf3.shape[0]
    nt = d3.shape[0]
    td = n // nt
    return pl.pallas_call(
        _dispatch_kernel, grid=(nt,), name="dispatch",
        in_specs=[pl.BlockSpec((1, 1, TOP_K * td), lambda i: (i, 0, 0), memory_space=pltpu.SMEM),
                  pl.BlockSpec(memory_space=pl.ANY)],
        out_specs=pl.BlockSpec(memory_space=pl.ANY),
        out_shape=jax.ShapeDtypeStruct((TOP_K * n, SUBLANES, LANES), jnp.float32),
        scratch_shapes=[pltpu.VMEM((DISPATCH_SLOTS, td, SUBLANES, LANES), jnp.float32),
                        pltpu.SemaphoreType.DMA((DISPATCH_SLOTS,)),
                        pltpu.SemaphoreType.DMA((DISPATCH_SLOTS,))],
        compiler_params=pltpu.CompilerParams(dimension_semantics=("arbitrary",)),
    )(d3, hf3)


def _expert_kernel(blk_ref, exp_ref, lo_ref, hi_ref, first_ref, last_ref, newexp_ref,
                   nitems_ref,
                   xs_hbm, wg_ref, wu_ref, wd_ref, ys_hbm,
                   xbuf_ref, yacc_ref, wgb_ref, wub_ref, wdb_ref, in_sem, out_sem):
    i = pl.program_id(0)
    nitems = nitems_ref[0]
    bm = xbuf_ref.shape[2]
    nslab = xbuf_ref.shape[1]

    def x_copies(item, start):
        b = blk_ref[item]
        for j in range(nslab):
            cp = pltpu.make_async_copy(xs_hbm.at[pl.ds(b * bm, bm), j],
                                       xbuf_ref.at[item % 2, j], in_sem.at[item % 2])
            cp.start() if start else cp.wait()

    def y_copies(b, start):
        for j in range(nslab):
            cp = pltpu.make_async_copy(yacc_ref.at[b % 2, j],
                                       ys_hbm.at[pl.ds(b * bm, bm), j], out_sem.at[b % 2])
            cp.start() if start else cp.wait()

    @pl.when(i == 0)
    def _():
        x_copies(0, True)

    @pl.when(i + 1 < nitems)
    def _():
        x_copies(i + 1, True)

    @pl.when((i < nitems) & (newexp_ref[i] == 1))
    def _():
        wgb_ref[...] = wg_ref[0].astype(jnp.bfloat16)
        wub_ref[...] = wu_ref[0].astype(jnp.bfloat16)
        wdb_ref[...] = wd_ref[0].astype(jnp.bfloat16)

    @pl.when(i < nitems)
    def _():
        b = blk_ref[i]
        x_copies(i, False)
        xb = jnp.concatenate([xbuf_ref[i % 2, j] for j in range(nslab)],
                             axis=1).astype(jnp.bfloat16)
        hg = jnp.dot(xb, wgb_ref[...], preferred_element_type=jnp.float32)
        hu = jnp.dot(xb, wub_ref[...], preferred_element_type=jnp.float32)
        h = hg * jax.nn.sigmoid(hg) * hu
        y = jnp.dot(h.astype(jnp.bfloat16), wdb_ref[...], preferred_element_type=jnp.float32)
        row = lax.broadcasted_iota(jnp.int32, (bm, 1), 0)
        y = jnp.where((row >= lo_ref[i]) & (row < hi_ref[i]), y, 0.0)

        @pl.when(first_ref[i] == 1)
        def _():
            @pl.when(b >= 2)
            def _():
                y_copies(b - 2, False)

            for j in range(nslab):
                yacc_ref[b % 2, j] = y[:, j * LANES:(j + 1) * LANES]

        @pl.when(first_ref[i] == 0)
        def _():
            for j in range(nslab):
                yacc_ref[b % 2, j] += y[:, j * LANES:(j + 1) * LANES]

        @pl.when(last_ref[i] == 1)
        def _():
            y_copies(b, True)

        @pl.when(i == nitems - 1)
        def _():
            y_copies(b, False)

            @pl.when(b >= 1)
            def _():
                y_copies(b - 1, False)


def _experts(xs3, items, wg, wu, wd, bm):
    rows = xs3.shape[0]
    ni = items[0].shape[0]
    d, de = wg.shape[1], wg.shape[2]
    nslab = d // LANES
    wsp = lambda shape: pl.BlockSpec(shape, lambda i, b, e, *_: (e[i], 0, 0))
    grid_spec = pltpu.PrefetchScalarGridSpec(
        num_scalar_prefetch=len(items), grid=(ni,),
        in_specs=[pl.BlockSpec(memory_space=pl.ANY),
                  wsp((1, d, de)), wsp((1, d, de)), wsp((1, de, d))],
        out_specs=pl.BlockSpec(memory_space=pl.ANY),
        scratch_shapes=[pltpu.VMEM((2, nslab, bm, LANES), jnp.float32),
                        pltpu.VMEM((2, nslab, bm, LANES), jnp.float32),
                        pltpu.VMEM((d, de), jnp.bfloat16), pltpu.VMEM((d, de), jnp.bfloat16),
                        pltpu.VMEM((de, d), jnp.bfloat16),
                        pltpu.SemaphoreType.DMA((2,)), pltpu.SemaphoreType.DMA((2,))])
    return pl.pallas_call(
        _expert_kernel, grid_spec=grid_spec, name="experts",
        out_shape=jax.ShapeDtypeStruct((rows, SUBLANES, LANES), jnp.float32),
        compiler_params=pltpu.CompilerParams(
            dimension_semantics=("arbitrary",), vmem_limit_bytes=VMEM_LIMIT_BYTES),
    )(*items, xs3, wg, wu, wd)


def _combine_kernel(dst_ref, dstn_ref, x1_ref, rg_ref, fw_ref, ys_hbm, o_ref, buf_ref, sem_ref):
    i = pl.program_id(0)
    n = pl.num_programs(0)
    tc = x1_ref.shape[0]
    slot = i % 2

    def gather(idx_ref, sl, start):
        for r0 in range(0, TOP_K * tc, INDEX_BATCH):
            rs = range(r0, r0 + INDEX_BATCH)
            idx = [idx_ref[0, 0, r] if start else 0 for r in rs]
            for r, ix in zip(rs, idx):
                cp = pltpu.make_async_copy(ys_hbm.at[ix],
                                           buf_ref.at[sl, pl.ds(SUBLANES * r, SUBLANES), :],
                                           sem_ref.at[sl])
                cp.start(priority=r % DMA_QUEUES) if start else cp.wait()

    @pl.when(i == 0)
    def _():
        gather(dst_ref, 0, True)

    @pl.when(i + 1 < n)
    def _():
        gather(dstn_ref, 1 - slot, True)

    gather(dst_ref, slot, False)
    g = rg_ref[...]
    cur = buf_ref.at[slot]
    y = (_load_tiled(cur, 0, tc) * g[:, 0:1]
         + _load_tiled(cur, SUBLANES * tc, tc) * g[:, 1:2])
    o_ref[...] = _rms(x1_ref[...] + y, fw_ref[...])


def _combine(x1, rg, ys3, d3, fw):
    n, d = x1.shape
    nt = d3.shape[0]
    tc = n // nt
    return pl.pallas_call(
        _combine_kernel, grid=(nt,), name="combine",
        in_specs=[
            pl.BlockSpec((1, 1, TOP_K * tc), lambda i: (i, 0, 0), memory_space=pltpu.SMEM),
            pl.BlockSpec((1, 1, TOP_K * tc), lambda i: (jnp.minimum(i + 1, nt - 1), 0, 0),
                         memory_space=pltpu.SMEM),
            pl.BlockSpec((tc, d), lambda i: (i, 0)),
            pl.BlockSpec((tc, LANES), lambda i: (i, 0)),
            pl.BlockSpec((1, d), lambda i: (0, 0)),
            pl.BlockSpec(memory_space=pl.ANY),
        ],
        out_specs=pl.BlockSpec((tc, d), lambda i: (i, 0)),
        out_shape=jax.ShapeDtypeStruct((n, d), jnp.float32),
        scratch_shapes=[pltpu.VMEM((2, TOP_K * tc * SUBLANES, LANES), jnp.float32),
                        pltpu.SemaphoreType.DMA((2,))],
        compiler_params=pltpu.CompilerParams(
            dimension_semantics=("arbitrary",), vmem_limit_bytes=VMEM_LIMIT_BYTES),
    )(d3, d3, x1, rg, fw, ys3)


def _rope_tables(positions):
    half = ROT_DIM // 2
    inv_freq = jnp.power(ROPE_THETA, -jnp.arange(half, dtype=jnp.float32) * 2.0 / ROT_DIM)
    ang = positions.astype(jnp.float32)[..., None] * jnp.tile(inv_freq, 2)
    cs = jnp.where(jnp.arange(2 * half) < half, jnp.cos(ang), jnp.sin(ang))
    j = jnp.arange(LANES) % HEAD_DIM
    f = jnp.arange(half)[:, None]
    hit = (j[None, :] < ROT_DIM) & ((j[None, :] % half) == f)
    sign = jnp.where(j < half, -1.0, 1.0)[None, :]
    zero = jnp.zeros((half, LANES), jnp.float32)
    sel = jnp.concatenate([jnp.concatenate([hit.astype(jnp.float32), zero], axis=1),
                           jnp.concatenate([zero, hit * sign], axis=1)], axis=0)
    return cs, jnp.tile(sel, (3, 1)).astype(jnp.bfloat16)


def _work_items(counts, bm, ni):
    end = jnp.cumsum(counts)
    start = end - counts
    fb = start // bm
    nblk = jnp.where(counts > 0, (end - 1) // bm - fb + 1, 0)
    iend = jnp.cumsum(nblk)
    nitems = iend[-1]
    i = jnp.minimum(jnp.arange(ni, dtype=jnp.int32), nitems - 1)
    e = jnp.sum((iend[None, :] <= i[:, None]).astype(jnp.int32), axis=1)
    e = jnp.minimum(e, N_EXPERTS - 1)
    hit = e[:, None] == jnp.arange(N_EXPERTS, dtype=jnp.int32)[None, :]
    pick = lambda table: jnp.sum(jnp.where(hit, table[None, :], 0), axis=1)
    blk = pick(fb) + i - pick(iend - nblk)
    lo = jnp.maximum(pick(start), blk * bm) - blk * bm
    hi = jnp.minimum(pick(end), (blk + 1) * bm) - blk * bm
    one = jnp.ones((1,), jnp.int32)
    first = jnp.concatenate([one, (blk[1:] != blk[:-1]).astype(jnp.int32)])
    last = jnp.concatenate([(blk[1:] != blk[:-1]).astype(jnp.int32), one])
    last = jnp.where(jnp.arange(ni) == nitems - 1, 1, last)
    newexp = jnp.concatenate([one, (e[1:] != e[:-1]).astype(jnp.int32)])
    as32 = lambda a: a.astype(jnp.int32)
    return start, (as32(blk), as32(e), as32(lo), as32(hi), first, as32(last), newexp,
                   as32(nitems).reshape(1))


def _layer(x, cs, sel, l, attn_norm_w, w_in, attn_sinks, conv_dw_w, conv_dw_b, conv_ln_w,
           conv_ln_b, attn_out_norm_w, conv_out_norm_w, w_out, ffn_norm_w, router_group_w,
           router_group_b, router_expert_w, router_expert_b, w_gate, w_up, w_down, out_norm_w):
    bsz, seq, d = x.shape
    n = bsz * seq
    assert d == SUBLANES * LANES, "token-tiled rows assume a 1024-wide model"
    bf = jnp.bfloat16
    row = lambda a: a.reshape(1, -1)
    wr = jnp.zeros((d, LANES), jnp.float32)
    wr = wr.at[:, :N_GROUPS].set(router_group_w[l]).at[:, N_GROUPS:N_GROUPS + N_EXPERTS].set(
        router_expert_w[l])
    wr_hi = wr.astype(bf)
    wr_lo = (wr - wr_hi.astype(jnp.float32)).astype(bf)
    br = jnp.zeros((1, LANES), jnp.float32)
    br = br.at[0, :N_GROUPS].set(router_group_b[l]).at[0, N_GROUPS:N_GROUPS + N_EXPERTS].set(
        router_expert_b[l])
    ts = min(SEQ_TILE, seq)
    params = dict(
        sinks=attn_sinks[l], sel=sel, anw=row(attn_norm_w[l]), win=w_in[l].astype(bf),
        dww=conv_dw_w[l], dwb=row(conv_dw_b[l]), lnw=row(conv_ln_w[l]), lnb=row(conv_ln_b[l]),
        aonw=row(attn_out_norm_w[l]), conw=row(conv_out_norm_w[l]), wout=w_out[l].astype(bf),
        fnw=row(ffn_norm_w[l]), wr=jnp.stack([wr_hi, wr_lo]), br=br,
        tri=jnp.tri(ts, k=-1, dtype=bf))
    x1, hf2, ri, rg, cnt = _mix(x, cs, params)

    bm = min(EXPERT_BLOCK, n * TOP_K)
    ni = (n * TOP_K) // bm + N_EXPERTS - 1
    counts = cnt[0, :N_EXPERTS].astype(jnp.int32)
    start, items = _work_items(counts, bm, ni)
    nt = n // ts
    hit = ri[:, 0:TOP_K, :, None] == jnp.arange(N_EXPERTS, dtype=jnp.int32)
    seg = jnp.sum(jnp.where(hit, start.astype(jnp.int32), 0), axis=-1)
    d3 = (seg + ri[:, TOP_K:2 * TOP_K, :]).reshape(nt, 1, TOP_K * ts)

    xs3 = _dispatch(hf2.reshape(n, SUBLANES, LANES), d3)
    ys3 = _experts(xs3, items, w_gate[l], w_up[l], w_down[l], bm)
    out = _combine(x1.reshape(n, d), rg.reshape(n, LANES), ys3, d3, row(out_norm_w))
    return out.reshape(bsz, seq, d)


def kernel(x, positions, attn_norm_w, w_in, attn_sinks, conv_dw_w, conv_dw_b, conv_ln_w, conv_ln_b, attn_out_norm_w, conv_out_norm_w, w_out, ffn_norm_w, router_group_w, router_group_b, router_expert_w, router_expert_b, w_gate, w_up, w_down, final_norm_w):
    depth = attn_norm_w.shape[0]
    assert depth == 1, "the combine kernel applies the final norm, so exactly one layer is fused"
    cs, sel = _rope_tables(positions)
    return _layer(x, cs, sel, 0, attn_norm_w, w_in, attn_sinks, conv_dw_w, conv_dw_b,
                  conv_ln_w, conv_ln_b, attn_out_norm_w, conv_out_norm_w, w_out, ffn_norm_w,
                  router_group_w, router_group_b, router_expert_w, router_expert_b,
                  w_gate, w_up, w_down, final_norm_w)
```

```python
import jax
import jax.numpy as jnp
from jax import lax
from jax.experimental import pallas as pl
from jax.experimental.pallas import tpu as pltpu

HEAD_DIM = 64
N_Q_HEADS = 8
N_KV_HEADS = 2
ATTN_WIDTH = N_Q_HEADS * HEAD_DIM
KV_WIDTH = N_KV_HEADS * HEAD_DIM
CONV_WIDTH = 512
CONV_KERNEL = 31
WINDOW = 128
ROPE_THETA = 500000.0
ROT_DIM = HEAD_DIM // 4
N_GROUPS = 4
EXPERTS_PER_GROUP = 8
N_EXPERTS = N_GROUPS * EXPERTS_PER_GROUP
TOP_K = 2
EPS = 1e-5

LANES = 128
SUBLANES = 8
VMEM_LIMIT_BYTES = 56 * 1024 * 1024

SEQ_TILE = 512
CONV_HALO = 32
CONV_CHUNK = 64
EXPERT_BLOCK = 512
DISPATCH_SLOTS = 3
EXPERT_IN_SLOTS = 3
INDEX_BATCH = 16
DMA_QUEUES = 2
NEG_BIG = -1e30

Q_END = ATTN_WIDTH
K_END = Q_END + KV_WIDTH
V_END = K_END + KV_WIDTH
CA_END = V_END + CONV_WIDTH
IN_COLS = CA_END + CONV_WIDTH


def _rms(x, w):
    return x * lax.rsqrt(jnp.mean(x * x, axis=-1, keepdims=True) + EPS) * w


def _bdot(a, b):
    return jnp.dot(a.astype(jnp.bfloat16), b.astype(jnp.bfloat16),
                   preferred_element_type=jnp.float32)


def _bdot_t(a, b):
    return lax.dot_general(a.astype(jnp.bfloat16), b.astype(jnp.bfloat16),
                           (((1,), (1,)), ((), ())), preferred_element_type=jnp.float32)


def _load_tiled(ref, base, rows):
    d = SUBLANES * LANES
    return jnp.concatenate(
        [ref[pl.ds(base + j, rows, stride=SUBLANES), :] for j in range(d // LANES)], axis=1)


def _store_tiled(ref, val):
    for j in range(val.shape[1] // LANES):
        ref[pl.ds(j, val.shape[0], stride=SUBLANES), :] = val[:, j * LANES:(j + 1) * LANES]


def _mix_kernel(sinks_ref,
                x_ref, cs_ref, sel_ref, anw_ref, win_ref, dww_ref, dwb_ref, lnw_ref, lnb_ref,
                aonw_ref, conw_ref, wout_ref, fnw_ref, wr_ref, br_ref, tri_ref,
                x1_ref, hf_ref, ri_ref, rg_ref, cnt_ref,
                kd_ref, vd_ref, ubuf_ref, ush_ref, attn_ref, run_ref):
    b = pl.program_id(0)
    s = pl.program_id(1)
    ts = x_ref.shape[1]
    nblk = ts // WINDOW
    lane = lax.broadcasted_iota(jnp.int32, (ts, LANES), 1)

    @pl.when(s == 0)
    def _():
        kd_ref[:, 0:WINDOW, :] = jnp.zeros((4, WINDOW, LANES), jnp.bfloat16)
        vd_ref[:, 0:WINDOW, :] = jnp.zeros((4, WINDOW, LANES), jnp.bfloat16)
        ubuf_ref[0:CONV_HALO, :] = jnp.zeros((CONV_HALO, CONV_WIDTH), jnp.float32)

    @pl.when((s == 0) & (b == 0))
    def _():
        run_ref[...] = jnp.zeros_like(run_ref)

    x = x_ref[0]
    hn = _rms(x, anw_ref[...]).astype(jnp.bfloat16)
    proj_c = _bdot(hn, win_ref[:, V_END:IN_COLS])

    u = proj_c[:, 0:CONV_WIDTH] * jax.nn.sigmoid(proj_c[:, CONV_WIDTH:2 * CONV_WIDTH])
    ubuf_ref[CONV_HALO:, :] = u
    span = ush_ref.shape[1]
    for sh in range(1, SUBLANES):
        ush_ref[sh - 1] = ubuf_ref[sh:sh + span, :]
    dwb = dwb_ref[...]
    off0 = CONV_HALO - (CONV_KERNEL - 1)
    convs = []
    for c in range(ts // CONV_CHUNK):
        acc = jnp.broadcast_to(dwb, (CONV_CHUNK, CONV_WIDTH))
        for t in range(CONV_KERNEL):
            a, sh = divmod(off0 + t, SUBLANES)
            r0 = c * CONV_CHUNK + a * SUBLANES
            win = (ubuf_ref[r0:r0 + CONV_CHUNK, :] if sh == 0
                   else ush_ref[sh - 1, r0:r0 + CONV_CHUNK, :])
            acc = acc + win * dww_ref[t:t + 1, :]
        convs.append(acc)
    conv = jnp.concatenate(convs, axis=0)
    ubuf_ref[0:CONV_HALO, :] = ubuf_ref[ts:ts + CONV_HALO, :]
    mu = jnp.mean(conv, axis=-1, keepdims=True)
    xc = conv - mu
    y = xc * lax.rsqrt(jnp.mean(xc * xc, axis=-1, keepdims=True) + EPS) * lnw_ref[...] + lnb_ref[...]
    conv = y * jax.nn.sigmoid(y)
    mixed_c = _rms(conv, conw_ref[...]).astype(jnp.bfloat16)

    proj_kv = _bdot(hn, win_ref[:, Q_END:V_END])
    proj_q = _bdot(hn, win_ref[:, 0:Q_END])

    cs = cs_ref[0]
    cs_hi = cs.astype(jnp.bfloat16)
    cs_r = cs - cs_hi.astype(jnp.float32)
    cs_mid = cs_r.astype(jnp.bfloat16)
    cs_lo = (cs_r - cs_mid.astype(jnp.float32)).astype(jnp.bfloat16)
    tab = jnp.dot(jnp.concatenate([cs_hi, cs_mid, cs_lo], axis=1), sel_ref[...],
                  preferred_element_type=jnp.float32)
    rotary = (lane % HEAD_DIM) < ROT_DIM
    cosv = jnp.where(rotary, tab[:, 0:LANES], 1.0)
    sinv = tab[:, LANES:2 * LANES]
    first_half = (lane % HEAD_DIM) < (ROT_DIM // 2)

    def rope(t):
        partner = jnp.where(first_half, pltpu.roll(t, LANES - ROT_DIM // 2, 1),
                            pltpu.roll(t, ROT_DIM // 2, 1))
        return t * cosv + partner * sinv

    k = rope(proj_kv[:, 0:KV_WIDTH])
    v = proj_kv[:, KV_WIDTH:2 * KV_WIDTH]
    low = lane < HEAD_DIM
    for src, dst in ((k, kd_ref), (v, vd_ref)):
        rolled = pltpu.roll(src, HEAD_DIM, 1)
        zero = jnp.zeros_like(src)
        dst[0, WINDOW:, :] = jnp.where(low, src, zero).astype(jnp.bfloat16)
        dst[1, WINDOW:, :] = jnp.where(low, zero, rolled).astype(jnp.bfloat16)
        dst[2, WINDOW:, :] = jnp.where(low, rolled, zero).astype(jnp.bfloat16)
        dst[3, WINDOW:, :] = jnp.where(low, zero, src).astype(jnp.bfloat16)

    qi = lax.broadcasted_iota(jnp.int32, (WINDOW, 2 * WINDOW), 0)
    kj = lax.broadcasted_iota(jnp.int32, (WINDOW, 2 * WINDOW), 1)
    dist = qi + WINDOW - kj
    band = (dist >= 0) & (dist < WINDOW)
    first_lo = jnp.where(s == 0, WINDOW, 0)
    lane_q = lax.broadcasted_iota(jnp.int32, (WINDOW, LANES), 1)
    for p in range(N_Q_HEADS // 2):
        g = p // 2
        qs = rope(proj_q[:, p * LANES:(p + 1) * LANES]) * (HEAD_DIM ** -0.5)
        for j in range(nblk):
            rows = slice(j * WINDOW, (j + 1) * WINDOW)
            keys = slice(j * WINDOW, (j + 2) * WINDOW)
            valid = (band & (kj >= first_lo)) if j == 0 else band
            qb = qs[rows]
            out = None
            inv = []
            for half in range(2):
                sink = sinks_ref[2 * p + half]
                sc = _bdot_t(qb, kd_ref[2 * g + half, keys, :])
                sc = jnp.where(valid, sc, NEG_BIG)
                m = jnp.maximum(jnp.max(sc, axis=-1, keepdims=True), sink)
                e = jnp.exp(sc - m)
                inv.append(1.0 / (jnp.sum(e, axis=-1, keepdims=True) + jnp.exp(sink - m)))
                o = _bdot(e, vd_ref[2 * g + half, keys, :])
                out = o if out is None else out + o
            attn_ref[rows, p * LANES:(p + 1) * LANES] = out * jnp.where(
                lane_q < HEAD_DIM, inv[0], inv[1])
    for c in range(4):
        kd_ref[c, 0:WINDOW, :] = kd_ref[c, ts:ts + WINDOW, :]
        vd_ref[c, 0:WINDOW, :] = vd_ref[c, ts:ts + WINDOW, :]

    mixed_a = _rms(attn_ref[...], aonw_ref[...])
    x1 = x + _bdot(mixed_a, wout_ref[0:ATTN_WIDTH, :]) + _bdot(mixed_c, wout_ref[ATTN_WIDTH:, :])
    x1_ref[0] = x1
    hf = _rms(x1, fnw_ref[...])
    _store_tiled(hf_ref, hf)

    hf_hi = hf.astype(jnp.bfloat16)
    hf_lo = (hf - hf_hi.astype(jnp.float32)).astype(jnp.bfloat16)
    both = jnp.dot(hf_hi, wr_ref[...], preferred_element_type=jnp.float32)
    logits = (both[:, 0:LANES] + both[:, LANES:2 * LANES]
              + jnp.dot(hf_lo, wr_ref[:, 0:LANES], preferred_element_type=jnp.float32)
              + br_ref[...])

    lanef = lane.astype(jnp.float32)
    big = float(LANES)
    gmask = lane < N_GROUPS
    gmax = jnp.max(jnp.where(gmask, logits, -jnp.inf), axis=-1, keepdims=True)
    gidx = jnp.min(jnp.where(gmask & (logits == gmax), lanef, big), axis=-1, keepdims=True)
    gsum = jnp.sum(jnp.where(gmask, jnp.exp(logits - gmax), 0.0), axis=-1, keepdims=True)
    g_p = 1.0 / gsum
    elo = N_GROUPS + EXPERTS_PER_GROUP * gidx
    emask = (lanef >= elo) & (lanef < elo + EXPERTS_PER_GROUP)
    em1 = jnp.max(jnp.where(emask, logits, -jnp.inf), axis=-1, keepdims=True)
    i1 = jnp.min(jnp.where(emask & (logits == em1), lanef, big), axis=-1, keepdims=True)
    mask2 = emask & (lanef != i1)
    em2 = jnp.max(jnp.where(mask2, logits, -jnp.inf), axis=-1, keepdims=True)
    i2 = jnp.min(jnp.where(mask2 & (logits == em2), lanef, big), axis=-1, keepdims=True)
    esum = jnp.sum(jnp.where(emask, jnp.exp(logits - em1), 0.0), axis=-1, keepdims=True)
    p1 = 1.0 / esum
    p2 = jnp.exp(em2 - em1) / esum
    gate1 = g_p * p1 / (p1 + p2)
    gate2 = g_p * p2 / (p1 + p2)
    e1 = i1 - N_GROUPS
    e2 = i2 - N_GROUPS

    oh1 = lanef == e1
    oh2 = lanef == e2
    onehot = jnp.where(oh1 | oh2, 1.0, 0.0)
    tot = jnp.dot(tri_ref[...], onehot.astype(jnp.bfloat16),
                  preferred_element_type=jnp.float32) + run_ref[0:1, :]
    r1 = jnp.sum(jnp.where(oh1, tot, 0.0), axis=-1, keepdims=True)
    r2 = jnp.sum(jnp.where(oh2, tot, 0.0), axis=-1, keepdims=True)
    run_ref[...] = run_ref[...] + jnp.sum(onehot, axis=0, keepdims=True)
    cnt_ref[...] = run_ref[...]

    ri = jnp.where(lane == 0, e1, jnp.where(lane == 1, e2, jnp.where(lane == 2, r1, r2)))
    ri_ref[0] = ri.T[0:SUBLANES, :].astype(jnp.int32)
    rg_ref[0] = jnp.where(lane == 0, gate1, gate2)


def _mix(x, cs, p):
    bsz, seq, d = x.shape
    ts = min(SEQ_TILE, seq)
    nst = seq // ts
    grid = (bsz, nst)
    tile = lambda last: pl.BlockSpec((1, ts, last), lambda b, s, *_: (b, s, 0))
    full = lambda a: pl.BlockSpec(a.shape, lambda b, s, *_: (0,) * a.ndim)
    weights = [p["sel"], p["anw"], p["win"], p["dww"], p["dwb"], p["lnw"], p["lnb"], p["aonw"], p["conw"],
               p["wout"], p["fnw"], p["wr"], p["br"], p["tri"]]
    out_shape = [
        jax.ShapeDtypeStruct((bsz, seq, d), jnp.float32),
        jax.ShapeDtypeStruct((bsz * seq * SUBLANES, LANES), jnp.float32),
        jax.ShapeDtypeStruct((bsz * nst, SUBLANES, ts), jnp.int32),
        jax.ShapeDtypeStruct((bsz, seq, LANES), jnp.float32),
        jax.ShapeDtypeStruct((SUBLANES, LANES), jnp.float32),
    ]
    out_specs = [tile(d),
                 pl.BlockSpec((ts * SUBLANES, LANES), lambda b, s, *_: (b * nst + s, 0)),
                 pl.BlockSpec((1, SUBLANES, ts), lambda b, s, *_: (b * nst + s, 0, 0)),
                 tile(LANES),
                 pl.BlockSpec((SUBLANES, LANES), lambda b, s, *_: (0, 0))]
    grid_spec = pltpu.PrefetchScalarGridSpec(
        num_scalar_prefetch=1, grid=grid,
        in_specs=[tile(d), tile(cs.shape[2])] + [full(w) for w in weights],
        out_specs=out_specs,
        scratch_shapes=[
            pltpu.VMEM((4, ts + WINDOW, LANES), jnp.bfloat16),
            pltpu.VMEM((4, ts + WINDOW, LANES), jnp.bfloat16),
            pltpu.VMEM((ts + CONV_HALO, CONV_WIDTH), jnp.float32),
            pltpu.VMEM((SUBLANES - 1, ts + CONV_HALO - SUBLANES, CONV_WIDTH), jnp.float32),
            pltpu.VMEM((ts, ATTN_WIDTH), jnp.float32),
            pltpu.VMEM((SUBLANES, LANES), jnp.float32),
        ])
    return pl.pallas_call(
        _mix_kernel, grid_spec=grid_spec, out_shape=out_shape, name="mix",
        compiler_params=pltpu.CompilerParams(
            dimension_semantics=("arbitrary", "arbitrary"),
            vmem_limit_bytes=VMEM_LIMIT_BYTES),
    )(p["sinks"], x, cs, *weights)


def _dispatch_kernel(dst_ref, hf_hbm, xs_hbm, buf_ref, in_sem, out_sem):
    i = pl.program_id(0)
    n = pl.num_programs(0)
    td = buf_ref.shape[1]
    nslot = buf_ref.shape[0]

    def tile_in(t):
        return pltpu.make_async_copy(hf_hbm.at[pl.ds(t * td, td)], buf_ref.at[t % nslot],
                                     in_sem.at[t % nslot])

    def rows(slot, start):
        src = buf_ref.at[slot]
        for q0 in range(0, TOP_K * td, INDEX_BATCH):
            qs = range(q0, q0 + INDEX_BATCH)
            idx = [dst_ref[0, 0, q] if start else 0 for q in qs]
            for q, ix in zip(qs, idx):
                cp = pltpu.make_async_copy(src.at[q % td], xs_hbm.at[ix], out_sem.at[slot])
                cp.start(priority=q % DMA_QUEUES) if start else cp.wait()

    @pl.when(i == 0)
    def _():
        tile_in(0).start()

    @pl.when((i == 0) & (n > 1))
    def _():
        tile_in(1).start()

    tile_in(i).wait()
    rows(i % nslot, True)

    @pl.when(i > 0)
    def _():
        rows((i - 1) % nslot, False)

    @pl.when(i + 2 < n)
    def _():
        tile_in(i + 2).start()

    @pl.when(i == n - 1)
    def _():
        rows(i % nslot, False)


def _dispatch(hf3, d3):
    n = hf3.shape[0]
    nt = d3.shape[0]
    td = n // nt
    return pl.pallas_call(
        _dispatch_kernel, grid=(nt,), name="dispatch",
        in_specs=[pl.BlockSpec((1, 1, TOP_K * td), lambda i: (i, 0, 0), memory_space=pltpu.SMEM),
                  pl.BlockSpec(memory_space=pl.ANY)],
        out_specs=pl.BlockSpec(memory_space=pl.ANY),
        out_shape=jax.ShapeDtypeStruct((TOP_K * n, SUBLANES, LANES), jnp.float32),
        scratch_shapes=[pltpu.VMEM((DISPATCH_SLOTS, td, SUBLANES, LANES), jnp.float32),
                        pltpu.SemaphoreType.DMA((DISPATCH_SLOTS,)),
                        pltpu.SemaphoreType.DMA((DISPATCH_SLOTS,))],
        compiler_params=pltpu.CompilerParams(dimension_semantics=("arbitrary",)),
    )(d3, hf3)


def _expert_kernel(blk_ref, exp_ref, lo_ref, hi_ref, first_ref, last_ref, newexp_ref,
                   nitems_ref,
                   xs_hbm, wg_ref, wu_ref, wd_ref, ys_hbm,
                   xbuf_ref, yacc_ref, wgb_ref, wub_ref, wdb_ref, in_sem, out_sem):
    i = pl.program_id(0)
    nitems = nitems_ref[0]
    bm = xbuf_ref.shape[2]
    nslab = xbuf_ref.shape[1]

    nxs = xbuf_ref.shape[0]

    def x_copies(item, start):
        b = blk_ref[item]
        for j in range(nslab):
            cp = pltpu.make_async_copy(xs_hbm.at[pl.ds(b * bm, bm), j],
                                       xbuf_ref.at[item % nxs, j], in_sem.at[item % nxs])
            cp.start() if start else cp.wait()

    def y_copies(b, start):
        for j in range(nslab):
            cp = pltpu.make_async_copy(yacc_ref.at[b % 2, j],
                                       ys_hbm.at[pl.ds(b * bm, bm), j], out_sem.at[b % 2])
            cp.start() if start else cp.wait()

    @pl.when(i == 0)
    def _():
        x_copies(0, True)

    @pl.when((i == 0) & (nitems > 1))
    def _():
        x_copies(1, True)

    @pl.when(i + 2 < nitems)
    def _():
        x_copies(i + 2, True)

    @pl.when((i < nitems) & (newexp_ref[i] == 1))
    def _():
        wgb_ref[...] = wg_ref[0].astype(jnp.bfloat16)
        wub_ref[...] = wu_ref[0].astype(jnp.bfloat16)
        wdb_ref[...] = wd_ref[0].astype(jnp.bfloat16)

    @pl.when(i < nitems)
    def _():
        b = blk_ref[i]
        x_copies(i, False)
        xb = jnp.concatenate([xbuf_ref[i % nxs, j] for j in range(nslab)],
                             axis=1).astype(jnp.bfloat16)
        hg = jnp.dot(xb, wgb_ref[...], preferred_element_type=jnp.float32)
        hu = jnp.dot(xb, wub_ref[...], preferred_element_type=jnp.float32)
        h = hg * jax.nn.sigmoid(hg) * hu
        y = jnp.dot(h.astype(jnp.bfloat16), wdb_ref[...], preferred_element_type=jnp.float32)
        row = lax.broadcasted_iota(jnp.int32, (bm, 1), 0)
        y = jnp.where((row >= lo_ref[i]) & (row < hi_ref[i]), y, 0.0)

        @pl.when(first_ref[i] == 1)
        def _():
            @pl.when(b >= 2)
            def _():
                y_copies(b - 2, False)

            for j in range(nslab):
                yacc_ref[b % 2, j] = y[:, j * LANES:(j + 1) * LANES]

        @pl.when(first_ref[i] == 0)
        def _():
            for j in range(nslab):
                yacc_ref[b % 2, j] += y[:, j * LANES:(j + 1) * LANES]

        @pl.when(last_ref[i] == 1)
        def _():
            y_copies(b, True)

        @pl.when(i == nitems - 1)
        def _():
            y_copies(b, False)

            @pl.when(b >= 1)
            def _():
                y_copies(b - 1, False)


def _experts(xs3, items, wg, wu, wd, bm):
    rows = xs3.shape[0]
    ni = items[0].shape[0]
    d, de = wg.shape[1], wg.shape[2]
    nslab = d // LANES
    wsp = lambda shape: pl.BlockSpec(shape, lambda i, b, e, *_: (e[i], 0, 0))
    grid_spec = pltpu.PrefetchScalarGridSpec(
        num_scalar_prefetch=len(items), grid=(ni,),
        in_specs=[pl.BlockSpec(memory_space=pl.ANY),
                  wsp((1, d, de)), wsp((1, d, de)), wsp((1, de, d))],
        out_specs=pl.BlockSpec(memory_space=pl.ANY),
        scratch_shapes=[pltpu.VMEM((EXPERT_IN_SLOTS, nslab, bm, LANES), jnp.float32),
                        pltpu.VMEM((2, nslab, bm, LANES), jnp.float32),
                        pltpu.VMEM((d, de), jnp.bfloat16), pltpu.VMEM((d, de), jnp.bfloat16),
                        pltpu.VMEM((de, d), jnp.bfloat16),
                        pltpu.SemaphoreType.DMA((EXPERT_IN_SLOTS,)),
                        pltpu.SemaphoreType.DMA((2,))])
    return pl.pallas_call(
        _expert_kernel, grid_spec=grid_spec, name="experts",
        out_shape=jax.ShapeDtypeStruct((rows, SUBLANES, LANES), jnp.float32),
        compiler_params=pltpu.CompilerParams(
            dimension_semantics=("arbitrary",), vmem_limit_bytes=VMEM_LIMIT_BYTES),
    )(*items, xs3, wg, wu, wd)


def _combine_kernel(dst_ref, dstn_ref, x1_ref, rg_ref, fw_ref, ys_hbm, o_ref, buf_ref, sem_ref):
    i = pl.program_id(0)
    n = pl.num_programs(0)
    tc = x1_ref.shape[0]
    slot = i % 2

    def gather(idx_ref, sl, start):
        for r0 in range(0, TOP_K * tc, INDEX_BATCH):
            rs = range(r0, r0 + INDEX_BATCH)
            idx = [idx_ref[0, 0, r] if start else 0 for r in rs]
            for r, ix in zip(rs, idx):
                cp = pltpu.make_async_copy(ys_hbm.at[ix],
                                           buf_ref.at[sl, pl.ds(SUBLANES * r, SUBLANES), :],
                                           sem_ref.at[sl])
                cp.start(priority=r % DMA_QUEUES) if start else cp.wait()

    @pl.when(i == 0)
    def _():
        gather(dst_ref, 0, True)

    @pl.when(i + 1 < n)
    def _():
        gather(dstn_ref, 1 - slot, True)

    gather(dst_ref, slot, False)
    g = rg_ref[...]
    cur = buf_ref.at[slot]
    y = (_load_tiled(cur, 0, tc) * g[:, 0:1]
         + _load_tiled(cur, SUBLANES * tc, tc) * g[:, 1:2])
    o_ref[...] = _rms(x1_ref[...] + y, fw_ref[...])


def _combine(x1, rg, ys3, d3, fw):
    n, d = x1.shape
    nt = d3.shape[0]
    tc = n // nt
    return pl.pallas_call(
        _combine_kernel, grid=(nt,), name="combine",
        in_specs=[
            pl.BlockSpec((1, 1, TOP_K * tc), lambda i: (i, 0, 0), memory_space=pltpu.SMEM),
            pl.BlockSpec((1, 1, TOP_K * tc), lambda i: (jnp.minimum(i + 1, nt - 1), 0, 0),
                         memory_space=pltpu.SMEM),
            pl.BlockSpec((tc, d), lambda i: (i, 0)),
            pl.BlockSpec((tc, LANES), lambda i: (i, 0)),
            pl.BlockSpec((1, d), lambda i: (0, 0)),
            pl.BlockSpec(memory_space=pl.ANY),
        ],
        out_specs=pl.BlockSpec((tc, d), lambda i: (i, 0)),
        out_shape=jax.ShapeDtypeStruct((n, d), jnp.float32),
        scratch_shapes=[pltpu.VMEM((2, TOP_K * tc * SUBLANES, LANES), jnp.float32),
                        pltpu.SemaphoreType.DMA((2,))],
        compiler_params=pltpu.CompilerParams(
            dimension_semantics=("arbitrary",), vmem_limit_bytes=VMEM_LIMIT_BYTES),
    )(d3, d3, x1, rg, fw, ys3)


def _rope_tables(positions):
    half = ROT_DIM // 2
    inv_freq = jnp.power(ROPE_THETA, -jnp.arange(half, dtype=jnp.float32) * 2.0 / ROT_DIM)
    ang = positions.astype(jnp.float32)[..., None] * inv_freq
    cs = jnp.concatenate([jnp.cos(ang), jnp.sin(ang)], axis=-1)
    j = jnp.arange(LANES) % HEAD_DIM
    f = jnp.arange(half)[:, None]
    hit = (j[None, :] < ROT_DIM) & ((j[None, :] % half) == f)
    sign = jnp.where(j < half, -1.0, 1.0)[None, :]
    zero = jnp.zeros((half, LANES), jnp.float32)
    sel = jnp.concatenate([jnp.concatenate([hit.astype(jnp.float32), zero], axis=1),
                           jnp.concatenate([zero, hit * sign], axis=1)], axis=0)
    return cs, jnp.tile(sel, (3, 1)).astype(jnp.bfloat16)


def _work_items(counts, bm, ni):
    end = jnp.cumsum(counts)
    start = end - counts
    fb = start // bm
    nblk = jnp.where(counts > 0, (end - 1) // bm - fb + 1, 0)
    iend = jnp.cumsum(nblk)
    nitems = iend[-1]
    i = jnp.minimum(jnp.arange(ni, dtype=jnp.int32), nitems - 1)
    e = jnp.sum((iend[None, :] <= i[:, None]).astype(jnp.int32), axis=1)
    e = jnp.minimum(e, N_EXPERTS - 1)
    hit = e[:, None] == jnp.arange(N_EXPERTS, dtype=jnp.int32)[None, :]
    pick = lambda table: jnp.sum(jnp.where(hit, table[None, :], 0), axis=1)
    blk = pick(fb) + i - pick(iend - nblk)
    lo = jnp.maximum(pick(start), blk * bm) - blk * bm
    hi = jnp.minimum(pick(end), (blk + 1) * bm) - blk * bm
    one = jnp.ones((1,), jnp.int32)
    first = jnp.concatenate([one, (blk[1:] != blk[:-1]).astype(jnp.int32)])
    last = jnp.concatenate([(blk[1:] != blk[:-1]).astype(jnp.int32), one])
    last = jnp.where(jnp.arange(ni) == nitems - 1, 1, last)
    newexp = jnp.concatenate([one, (e[1:] != e[:-1]).astype(jnp.int32)])
    as32 = lambda a: a.astype(jnp.int32)
    return start, (as32(blk), as32(e), as32(lo), as32(hi), first, as32(last), newexp,
                   as32(nitems).reshape(1))


def _layer(x, cs, sel, l, attn_norm_w, w_in, attn_sinks, conv_dw_w, conv_dw_b, conv_ln_w,
           conv_ln_b, attn_out_norm_w, conv_out_norm_w, w_out, ffn_norm_w, router_group_w,
           router_group_b, router_expert_w, router_expert_b, w_gate, w_up, w_down, out_norm_w):
    bsz, seq, d = x.shape
    n = bsz * seq
    assert d == SUBLANES * LANES, "token-tiled rows assume a 1024-wide model"
    bf = jnp.bfloat16
    row = lambda a: a.reshape(1, -1)
    wr = jnp.zeros((d, LANES), jnp.float32)
    wr = wr.at[:, :N_GROUPS].set(router_group_w[l]).at[:, N_GROUPS:N_GROUPS + N_EXPERTS].set(
        router_expert_w[l])
    wr_hi = wr.astype(bf)
    wr_lo = (wr - wr_hi.astype(jnp.float32)).astype(bf)
    br = jnp.zeros((1, LANES), jnp.float32)
    br = br.at[0, :N_GROUPS].set(router_group_b[l]).at[0, N_GROUPS:N_GROUPS + N_EXPERTS].set(
        router_expert_b[l])
    ts = min(SEQ_TILE, seq)
    params = dict(
        sinks=attn_sinks[l], sel=sel, anw=row(attn_norm_w[l]), win=w_in[l].astype(bf),
        dww=conv_dw_w[l], dwb=row(conv_dw_b[l]), lnw=row(conv_ln_w[l]), lnb=row(conv_ln_b[l]),
        aonw=row(attn_out_norm_w[l]), conw=row(conv_out_norm_w[l]), wout=w_out[l].astype(bf),
        fnw=row(ffn_norm_w[l]), wr=jnp.concatenate([wr_hi, wr_lo], axis=1), br=br,
        tri=jnp.tri(ts, k=-1, dtype=bf))
    x1, hf2, ri, rg, cnt = _mix(x, cs, params)

    bm = min(EXPERT_BLOCK, n * TOP_K)
    ni = (n * TOP_K) // bm + N_EXPERTS - 1
    counts = cnt[0, :N_EXPERTS].astype(jnp.int32)
    start, items = _work_items(counts, bm, ni)
    nt = n // ts
    hit = ri[:, 0:TOP_K, :, None] == jnp.arange(N_EXPERTS, dtype=jnp.int32)
    seg = jnp.sum(jnp.where(hit, start.astype(jnp.int32), 0), axis=-1)
    d3 = (seg + ri[:, TOP_K:2 * TOP_K, :]).reshape(nt, 1, TOP_K * ts)

    xs3 = _dispatch(hf2.reshape(n, SUBLANES, LANES), d3)
    ys3 = _experts(xs3, items, w_gate[l], w_up[l], w_down[l], bm)
    out = _combine(x1.reshape(n, d), rg.reshape(n, LANES), ys3, d3, row(out_norm_w))
    return out.reshape(bsz, seq, d)


def kernel(x, positions, attn_norm_w, w_in, attn_sinks, conv_dw_w, conv_dw_b, conv_ln_w, conv_ln_b, attn_out_norm_w, conv_out_norm_w, w_out, ffn_norm_w, router_group_w, router_group_b, router_expert_w, router_expert_b, w_gate, w_up, w_down, final_norm_w):
    depth = attn_norm_w.shape[0]
    assert depth == 1, "the combine kernel applies the final norm, so exactly one layer is fused"
    cs, sel = _rope_tables(positions)
    return _layer(x, cs, sel, 0, attn_norm_w, w_in, attn_sinks, conv_dw_w, conv_dw_b,
                  conv_ln_w, conv_ln_b, attn_out_norm_w, conv_out_norm_w, w_out, ffn_norm_w,
                  router_group_w, router_group_b, router_expert_w, router_expert_b,
                  w_gate, w_up, w_down, final_norm_w)
```

```python
import jax
import jax.numpy as jnp
from jax import lax
from jax.experimental import pallas as pl
from jax.experimental.pallas import tpu as pltpu

HEAD_DIM = 64
N_Q_HEADS = 8
N_KV_HEADS = 2
ATTN_WIDTH = N_Q_HEADS * HEAD_DIM
KV_WIDTH = N_KV_HEADS * HEAD_DIM
CONV_WIDTH = 512
CONV_KERNEL = 31
WINDOW = 128
ROPE_THETA = 500000.0
ROT_DIM = HEAD_DIM // 4
N_GROUPS = 4
EXPERTS_PER_GROUP = 8
N_EXPERTS = N_GROUPS * EXPERTS_PER_GROUP
TOP_K = 2
EPS = 1e-5

LANES = 128
SUBLANES = 8
VMEM_LIMIT_BYTES = 56 * 1024 * 1024

SEQ_TILE = 512
CONV_HALO = 32
CONV_CHUNK = 64
EXPERT_BLOCK = 512
DISPATCH_SLOTS = 3
EXPERT_IN_SLOTS = 3
INDEX_BATCH = 16
DMA_QUEUES = 2
NEG_BIG = -1e30
LOG2_E = 1.4426950408889634

Q_END = ATTN_WIDTH
K_END = Q_END + KV_WIDTH
V_END = K_END + KV_WIDTH
CA_END = V_END + CONV_WIDTH
IN_COLS = CA_END + CONV_WIDTH


def _rms(x, w):
    return x * lax.rsqrt(jnp.mean(x * x, axis=-1, keepdims=True) + EPS) * w


def _bdot(a, b):
    return jnp.dot(a.astype(jnp.bfloat16), b.astype(jnp.bfloat16),
                   preferred_element_type=jnp.float32)


def _bdot_t(a, b):
    return lax.dot_general(a.astype(jnp.bfloat16), b.astype(jnp.bfloat16),
                           (((1,), (1,)), ((), ())), preferred_element_type=jnp.float32)


def _load_tiled(ref, base, rows):
    d = SUBLANES * LANES
    return jnp.concatenate(
        [ref[pl.ds(base + j, rows, stride=SUBLANES), :] for j in range(d // LANES)], axis=1)


def _store_tiled(ref, val):
    for j in range(val.shape[1] // LANES):
        ref[pl.ds(j, val.shape[0], stride=SUBLANES), :] = val[:, j * LANES:(j + 1) * LANES]


def _mix_kernel(sinks_ref,
                x_ref, cs_ref, sel_ref, anw_ref, win_ref, dww_ref, dwb_ref, lnw_ref, lnb_ref,
                aonw_ref, conw_ref, wout_ref, fnw_ref, wr_ref, br_ref, tri_ref,
                x1_ref, hf_ref, ri_ref, rg_ref, cnt_ref,
                kd_ref, vd_ref, ubuf_ref, ush_ref, attn_ref, run_ref):
    b = pl.program_id(0)
    s = pl.program_id(1)
    ts = x_ref.shape[1]
    nblk = ts // WINDOW
    lane = lax.broadcasted_iota(jnp.int32, (ts, LANES), 1)

    @pl.when(s == 0)
    def _():
        kd_ref[:, 0:WINDOW, :] = jnp.zeros((4, WINDOW, LANES), jnp.bfloat16)
        vd_ref[:, 0:WINDOW, :] = jnp.zeros((4, WINDOW, LANES), jnp.bfloat16)
        ubuf_ref[0:CONV_HALO, :] = jnp.zeros((CONV_HALO, CONV_WIDTH), jnp.float32)

    @pl.when((s == 0) & (b == 0))
    def _():
        run_ref[...] = jnp.zeros_like(run_ref)

    x = x_ref[0]
    hn = _rms(x, anw_ref[...]).astype(jnp.bfloat16)
    proj_c = _bdot(hn, win_ref[:, V_END:IN_COLS])

    u = proj_c[:, 0:CONV_WIDTH] * jax.nn.sigmoid(proj_c[:, CONV_WIDTH:2 * CONV_WIDTH])
    ubuf_ref[CONV_HALO:, :] = u
    span = ush_ref.shape[1]
    for sh in range(1, SUBLANES):
        ush_ref[sh - 1] = ubuf_ref[sh:sh + span, :]
    dwb = dwb_ref[...]
    off0 = CONV_HALO - (CONV_KERNEL - 1)
    convs = []
    for c in range(ts // CONV_CHUNK):
        acc = jnp.broadcast_to(dwb, (CONV_CHUNK, CONV_WIDTH))
        for t in range(CONV_KERNEL):
            a, sh = divmod(off0 + t, SUBLANES)
            r0 = c * CONV_CHUNK + a * SUBLANES
            win = (ubuf_ref[r0:r0 + CONV_CHUNK, :] if sh == 0
                   else ush_ref[sh - 1, r0:r0 + CONV_CHUNK, :])
            acc = acc + win * dww_ref[t:t + 1, :]
        convs.append(acc)
    conv = jnp.concatenate(convs, axis=0)
    ubuf_ref[0:CONV_HALO, :] = ubuf_ref[ts:ts + CONV_HALO, :]
    mu = jnp.mean(conv, axis=-1, keepdims=True)
    xc = conv - mu
    y = xc * lax.rsqrt(jnp.mean(xc * xc, axis=-1, keepdims=True) + EPS) * lnw_ref[...] + lnb_ref[...]
    conv = y * jax.nn.sigmoid(y)
    mixed_c = _rms(conv, conw_ref[...]).astype(jnp.bfloat16)

    proj_kv = _bdot(hn, win_ref[:, Q_END:V_END])
    proj_q = _bdot(hn, win_ref[:, 0:Q_END])

    cs = cs_ref[0]
    cs_hi = cs.astype(jnp.bfloat16)
    cs_r = cs - cs_hi.astype(jnp.float32)
    cs_mid = cs_r.astype(jnp.bfloat16)
    cs_lo = (cs_r - cs_mid.astype(jnp.float32)).astype(jnp.bfloat16)
    tab = jnp.dot(jnp.concatenate([cs_hi, cs_mid, cs_lo], axis=1), sel_ref[...],
                  preferred_element_type=jnp.float32)
    rotary = (lane % HEAD_DIM) < ROT_DIM
    cosv = jnp.where(rotary, tab[:, 0:LANES], 1.0)
    sinv = tab[:, LANES:2 * LANES]
    first_half = (lane % HEAD_DIM) < (ROT_DIM // 2)

    def rope(t):
        partner = jnp.where(first_half, pltpu.roll(t, LANES - ROT_DIM // 2, 1),
                            pltpu.roll(t, ROT_DIM // 2, 1))
        return t * cosv + partner * sinv

    k = rope(proj_kv[:, 0:KV_WIDTH])
    v = proj_kv[:, KV_WIDTH:2 * KV_WIDTH]
    low = lane < HEAD_DIM
    for src, dst in ((k, kd_ref), (v, vd_ref)):
        rolled = pltpu.roll(src, HEAD_DIM, 1)
        zero = jnp.zeros_like(src)
        dst[0, WINDOW:, :] = jnp.where(low, src, zero).astype(jnp.bfloat16)
        dst[1, WINDOW:, :] = jnp.where(low, zero, rolled).astype(jnp.bfloat16)
        dst[2, WINDOW:, :] = jnp.where(low, rolled, zero).astype(jnp.bfloat16)
        dst[3, WINDOW:, :] = jnp.where(low, zero, src).astype(jnp.bfloat16)

    qi = lax.broadcasted_iota(jnp.int32, (WINDOW, 2 * WINDOW), 0)
    kj = lax.broadcasted_iota(jnp.int32, (WINDOW, 2 * WINDOW), 1)
    dist = qi + WINDOW - kj
    band = (dist >= 0) & (dist < WINDOW)
    first_lo = jnp.where(s == 0, WINDOW, 0)
    lane_q = lax.broadcasted_iota(jnp.int32, (WINDOW, LANES), 1)
    for p in range(N_Q_HEADS // 2):
        g = p // 2
        qs = rope(proj_q[:, p * LANES:(p + 1) * LANES]) * (HEAD_DIM ** -0.5 * LOG2_E)
        for j in range(nblk):
            rows = slice(j * WINDOW, (j + 1) * WINDOW)
            keys = slice(j * WINDOW, (j + 2) * WINDOW)
            valid = (band & (kj >= first_lo)) if j == 0 else band
            qb = qs[rows]
            out = None
            inv = []
            for half in range(2):
                sink = sinks_ref[2 * p + half] * LOG2_E
                sc = _bdot_t(qb, kd_ref[2 * g + half, keys, :])
                sc = jnp.where(valid, sc, NEG_BIG)
                m = jnp.maximum(jnp.max(sc, axis=-1, keepdims=True), sink)
                e = jnp.exp2(sc - m)
                inv.append(1.0 / (jnp.sum(e, axis=-1, keepdims=True) + jnp.exp2(sink - m)))
                o = _bdot(e, vd_ref[2 * g + half, keys, :])
                out = o if out is None else out + o
            attn_ref[rows, p * LANES:(p + 1) * LANES] = out * jnp.where(
                lane_q < HEAD_DIM, inv[0], inv[1])
    for c in range(4):
        kd_ref[c, 0:WINDOW, :] = kd_ref[c, ts:ts + WINDOW, :]
        vd_ref[c, 0:WINDOW, :] = vd_ref[c, ts:ts + WINDOW, :]

    mixed_a = _rms(attn_ref[...], aonw_ref[...])
    x1 = x + _bdot(mixed_a, wout_ref[0:ATTN_WIDTH, :]) + _bdot(mixed_c, wout_ref[ATTN_WIDTH:, :])
    x1_ref[0] = x1
    hf = _rms(x1, fnw_ref[...])
    _store_tiled(hf_ref, hf)

    hf_hi = hf.astype(jnp.bfloat16)
    hf_lo = (hf - hf_hi.astype(jnp.float32)).astype(jnp.bfloat16)
    both = jnp.dot(hf_hi, wr_ref[...], preferred_element_type=jnp.float32)
    logits = (both[:, 0:LANES] + both[:, LANES:2 * LANES]
              + jnp.dot(hf_lo, wr_ref[:, 0:LANES], preferred_element_type=jnp.float32)
              + br_ref[...])

    lanef = lane.astype(jnp.float32)
    big = float(LANES)
    gmask = lane < N_GROUPS
    gmax = jnp.max(jnp.where(gmask, logits, -jnp.inf), axis=-1, keepdims=True)
    gidx = jnp.min(jnp.where(gmask & (logits == gmax), lanef, big), axis=-1, keepdims=True)
    gsum = jnp.sum(jnp.where(gmask, jnp.exp(logits - gmax), 0.0), axis=-1, keepdims=True)
    g_p = 1.0 / gsum
    elo = N_GROUPS + EXPERTS_PER_GROUP * gidx
    emask = (lanef >= elo) & (lanef < elo + EXPERTS_PER_GROUP)
    em1 = jnp.max(jnp.where(emask, logits, -jnp.inf), axis=-1, keepdims=True)
    i1 = jnp.min(jnp.where(emask & (logits == em1), lanef, big), axis=-1, keepdims=True)
    mask2 = emask & (lanef != i1)
    em2 = jnp.max(jnp.where(mask2, logits, -jnp.inf), axis=-1, keepdims=True)
    i2 = jnp.min(jnp.where(mask2 & (logits == em2), lanef, big), axis=-1, keepdims=True)
    esum = jnp.sum(jnp.where(emask, jnp.exp(logits - em1), 0.0), axis=-1, keepdims=True)
    p1 = 1.0 / esum
    p2 = jnp.exp(em2 - em1) / esum
    gate1 = g_p * p1 / (p1 + p2)
    gate2 = g_p * p2 / (p1 + p2)
    e1 = i1 - N_GROUPS
    e2 = i2 - N_GROUPS

    oh1 = lanef == e1
    oh2 = lanef == e2
    onehot = jnp.where(oh1 | oh2, 1.0, 0.0)
    tot = jnp.dot(tri_ref[...], onehot.astype(jnp.bfloat16),
                  preferred_element_type=jnp.float32) + run_ref[0:1, :]
    r1 = jnp.sum(jnp.where(oh1, tot, 0.0), axis=-1, keepdims=True)
    r2 = jnp.sum(jnp.where(oh2, tot, 0.0), axis=-1, keepdims=True)
    run_ref[...] = run_ref[...] + jnp.sum(onehot, axis=0, keepdims=True)
    cnt_ref[...] = run_ref[...]

    ri = jnp.where(lane == 0, e1, jnp.where(lane == 1, e2, jnp.where(lane == 2, r1, r2)))
    ri_ref[0] = ri.T[0:SUBLANES, :].astype(jnp.int32)
    rg_ref[0] = jnp.where(lane == 0, gate1, gate2)


def _mix(x, cs, p):
    bsz, seq, d = x.shape
    ts = min(SEQ_TILE, seq)
    nst = seq // ts
    grid = (bsz, nst)
    tile = lambda last: pl.BlockSpec((1, ts, last), lambda b, s, *_: (b, s, 0))
    full = lambda a: pl.BlockSpec(a.shape, lambda b, s, *_: (0,) * a.ndim)
    weights = [p["sel"], p["anw"], p["win"], p["dww"], p["dwb"], p["lnw"], p["lnb"], p["aonw"], p["conw"],
               p["wout"], p["fnw"], p["wr"], p["br"], p["tri"]]
    out_shape = [
        jax.ShapeDtypeStruct((bsz, seq, d), jnp.float32),
        jax.ShapeDtypeStruct((bsz * seq * SUBLANES, LANES), jnp.float32),
        jax.ShapeDtypeStruct((bsz * nst, SUBLANES, ts), jnp.int32),
        jax.ShapeDtypeStruct((bsz, seq, LANES), jnp.float32),
        jax.ShapeDtypeStruct((SUBLANES, LANES), jnp.float32),
    ]
    out_specs = [tile(d),
                 pl.BlockSpec((ts * SUBLANES, LANES), lambda b, s, *_: (b * nst + s, 0)),
                 pl.BlockSpec((1, SUBLANES, ts), lambda b, s, *_: (b * nst + s, 0, 0)),
                 tile(LANES),
                 pl.BlockSpec((SUBLANES, LANES), lambda b, s, *_: (0, 0))]
    grid_spec = pltpu.PrefetchScalarGridSpec(
        num_scalar_prefetch=1, grid=grid,
        in_specs=[tile(d), tile(cs.shape[2])] + [full(w) for w in weights],
        out_specs=out_specs,
        scratch_shapes=[
            pltpu.VMEM((4, ts + WINDOW, LANES), jnp.bfloat16),
            pltpu.VMEM((4, ts + WINDOW, LANES), jnp.bfloat16),
            pltpu.VMEM((ts + CONV_HALO, CONV_WIDTH), jnp.float32),
            pltpu.VMEM((SUBLANES - 1, ts + CONV_HALO - SUBLANES, CONV_WIDTH), jnp.float32),
            pltpu.VMEM((ts, ATTN_WIDTH), jnp.float32),
            pltpu.VMEM((SUBLANES, LANES), jnp.float32),
        ])
    return pl.pallas_call(
        _mix_kernel, grid_spec=grid_spec, out_shape=out_shape, name="mix",
        compiler_params=pltpu.CompilerParams(
            dimension_semantics=("arbitrary", "arbitrary"),
            vmem_limit_bytes=VMEM_LIMIT_BYTES),
    )(p["sinks"], x, cs, *weights)


def _dispatch_kernel(dst_ref, hf_hbm, xs_hbm, buf_ref, in_sem, out_sem):
    i = pl.program_id(0)
    n = pl.num_programs(0)
    td = buf_ref.shape[1]
    nslot = buf_ref.shape[0]

    def tile_in(t):
        return pltpu.make_async_copy(hf_hbm.at[pl.ds(t * td, td)], buf_ref.at[t % nslot],
                                     in_sem.at[t % nslot])

    def rows(slot, start):
        src = buf_ref.at[slot]
        per = INDEX_BATCH // TOP_K
        for r0 in range(0, td, per):
            qs = [kk * td + r for r in range(r0, r0 + per) for kk in range(TOP_K)]
            idx = [dst_ref[0, 0, q] if start else 0 for q in qs]
            for n_q, (q, ix) in enumerate(zip(qs, idx)):
                cp = pltpu.make_async_copy(src.at[q % td], xs_hbm.at[ix], out_sem.at[slot])
                cp.start(priority=n_q % DMA_QUEUES) if start else cp.wait()

    @pl.when(i == 0)
    def _():
        tile_in(0).start()

    @pl.when((i == 0) & (n > 1))
    def _():
        tile_in(1).start()

    tile_in(i).wait()
    rows(i % nslot, True)

    @pl.when(i > 0)
    def _():
        rows((i - 1) % nslot, False)

    @pl.when(i + 2 < n)
    def _():
        tile_in(i + 2).start()

    @pl.when(i == n - 1)
    def _():
        rows(i % nslot, False)


def _dispatch(hf3, d3):
    n = hf3.shape[0]
    nt = d3.shape[0]
    td = n // nt
    return pl.pallas_call(
        _dispatch_kernel, grid=(nt,), name="dispatch",
        in_specs=[pl.BlockSpec((1, 1, TOP_K * td), lambda i: (i, 0, 0), memory_space=pltpu.SMEM),
                  pl.BlockSpec(memory_space=pl.ANY)],
        out_specs=pl.BlockSpec(memory_space=pl.ANY),
        out_shape=jax.ShapeDtypeStruct((TOP_K * n, SUBLANES, LANES), jnp.float32),
        scratch_shapes=[pltpu.VMEM((DISPATCH_SLOTS, td, SUBLANES, LANES), jnp.float32),
                        pltpu.SemaphoreType.DMA((DISPATCH_SLOTS,)),
                        pltpu.SemaphoreType.DMA((DISPATCH_SLOTS,))],
        compiler_params=pltpu.CompilerParams(dimension_semantics=("arbitrary",)),
    )(d3, hf3)


def _expert_kernel(blk_ref, exp_ref, lo_ref, hi_ref, first_ref, last_ref, newexp_ref,
                   nitems_ref,
                   xs_hbm, wg_ref, wu_ref, wd_ref, ys_hbm,
                   xbuf_ref, yacc_ref, wgb_ref, wub_ref, wdb_ref, in_sem, out_sem):
    i = pl.program_id(0)
    nitems = nitems_ref[0]
    bm = xbuf_ref.shape[2]
    nslab = xbuf_ref.shape[1]

    nxs = xbuf_ref.shape[0]

    def x_copies(item, start):
        b = blk_ref[item]
        for j in range(nslab):
            cp = pltpu.make_async_copy(xs_hbm.at[pl.ds(b * bm, bm), j],
                                       xbuf_ref.at[item % nxs, j], in_sem.at[item % nxs])
            cp.start() if start else cp.wait()

    def y_copies(b, start):
        for j in range(nslab):
            cp = pltpu.make_async_copy(yacc_ref.at[b % 2, j],
                                       ys_hbm.at[pl.ds(b * bm, bm), j], out_sem.at[b % 2])
            cp.start() if start else cp.wait()

    @pl.when(i == 0)
    def _():
        x_copies(0, True)

    @pl.when((i == 0) & (nitems > 1))
    def _():
        x_copies(1, True)

    @pl.when(i + 2 < nitems)
    def _():
        x_copies(i + 2, True)

    @pl.when((i < nitems) & (newexp_ref[i] == 1))
    def _():
        wgb_ref[...] = wg_ref[0].astype(jnp.bfloat16)
        wub_ref[...] = wu_ref[0].astype(jnp.bfloat16)
        wdb_ref[...] = wd_ref[0].astype(jnp.bfloat16)

    @pl.when(i < nitems)
    def _():
        b = blk_ref[i]
        x_copies(i, False)
        xb = jnp.concatenate([xbuf_ref[i % nxs, j] for j in range(nslab)],
                             axis=1).astype(jnp.bfloat16)
        hg = jnp.dot(xb, wgb_ref[...], preferred_element_type=jnp.float32)
        hu = jnp.dot(xb, wub_ref[...], preferred_element_type=jnp.float32)
        h = hg * jax.nn.sigmoid(hg) * hu
        y = jnp.dot(h.astype(jnp.bfloat16), wdb_ref[...], preferred_element_type=jnp.float32)
        row = lax.broadcasted_iota(jnp.int32, (bm, 1), 0)
        y = jnp.where((row >= lo_ref[i]) & (row < hi_ref[i]), y, 0.0)

        @pl.when(first_ref[i] == 1)
        def _():
            @pl.when(b >= 2)
            def _():
                y_copies(b - 2, False)

            for j in range(nslab):
                yacc_ref[b % 2, j] = y[:, j * LANES:(j + 1) * LANES]

        @pl.when(first_ref[i] == 0)
        def _():
            for j in range(nslab):
                yacc_ref[b % 2, j] += y[:, j * LANES:(j + 1) * LANES]

        @pl.when(last_ref[i] == 1)
        def _():
            y_copies(b, True)

        @pl.when(i == nitems - 1)
        def _():
            y_copies(b, False)

            @pl.when(b >= 1)
            def _():
                y_copies(b - 1, False)


def _experts(xs3, items, wg, wu, wd, bm):
    rows = xs3.shape[0]
    ni = items[0].shape[0]
    d, de = wg.shape[1], wg.shape[2]
    nslab = d // LANES
    wsp = lambda shape: pl.BlockSpec(shape, lambda i, b, e, *_: (e[i], 0, 0))
    grid_spec = pltpu.PrefetchScalarGridSpec(
        num_scalar_prefetch=len(items), grid=(ni,),
        in_specs=[pl.BlockSpec(memory_space=pl.ANY),
                  wsp((1, d, de)), wsp((1, d, de)), wsp((1, de, d))],
        out_specs=pl.BlockSpec(memory_space=pl.ANY),
        scratch_shapes=[pltpu.VMEM((EXPERT_IN_SLOTS, nslab, bm, LANES), jnp.float32),
                        pltpu.VMEM((2, nslab, bm, LANES), jnp.float32),
                        pltpu.VMEM((d, de), jnp.bfloat16), pltpu.VMEM((d, de), jnp.bfloat16),
                        pltpu.VMEM((de, d), jnp.bfloat16),
                        pltpu.SemaphoreType.DMA((EXPERT_IN_SLOTS,)),
                        pltpu.SemaphoreType.DMA((2,))])
    return pl.pallas_call(
        _expert_kernel, grid_spec=grid_spec, name="experts",
        out_shape=jax.ShapeDtypeStruct((rows, SUBLANES, LANES), jnp.float32),
        compiler_params=pltpu.CompilerParams(
            dimension_semantics=("arbitrary",), vmem_limit_bytes=VMEM_LIMIT_BYTES),
    )(*items, xs3, wg, wu, wd)


def _combine_kernel(dst_ref, dstn_ref, x1_ref, rg_ref, fw_ref, ys_hbm, o_ref, buf_ref, sem_ref):
    i = pl.program_id(0)
    n = pl.num_programs(0)
    tc = x1_ref.shape[0]
    slot = i % 2

    def gather(idx_ref, sl, start):
        for r0 in range(0, TOP_K * tc, INDEX_BATCH):
            rs = range(r0, r0 + INDEX_BATCH)
            idx = [idx_ref[0, 0, r] if start else 0 for r in rs]
            for r, ix in zip(rs, idx):
                cp = pltpu.make_async_copy(ys_hbm.at[ix],
                                           buf_ref.at[sl, pl.ds(SUBLANES * r, SUBLANES), :],
                                           sem_ref.at[sl])
                cp.start(priority=r % DMA_QUEUES) if start else cp.wait()

    @pl.when(i == 0)
    def _():
        gather(dst_ref, 0, True)

    @pl.when(i + 1 < n)
    def _():
        gather(dstn_ref, 1 - slot, True)

    gather(dst_ref, slot, False)
    g = rg_ref[...]
    cur = buf_ref.at[slot]
    y = (_load_tiled(cur, 0, tc) * g[:, 0:1]
         + _load_tiled(cur, SUBLANES * tc, tc) * g[:, 1:2])
    o_ref[...] = _rms(x1_ref[...] + y, fw_ref[...])


def _combine(x1, rg, ys3, d3, fw):
    n, d = x1.shape
    nt = d3.shape[0]
    tc = n // nt
    return pl.pallas_call(
        _combine_kernel, grid=(nt,), name="combine",
        in_specs=[
            pl.BlockSpec((1, 1, TOP_K * tc), lambda i: (i, 0, 0), memory_space=pltpu.SMEM),
            pl.BlockSpec((1, 1, TOP_K * tc), lambda i: (jnp.minimum(i + 1, nt - 1), 0, 0),
                         memory_space=pltpu.SMEM),
            pl.BlockSpec((tc, d), lambda i: (i, 0)),
            pl.BlockSpec((tc, LANES), lambda i: (i, 0)),
            pl.BlockSpec((1, d), lambda i: (0, 0)),
            pl.BlockSpec(memory_space=pl.ANY),
        ],
        out_specs=pl.BlockSpec((tc, d), lambda i: (i, 0)),
        out_shape=jax.ShapeDtypeStruct((n, d), jnp.float32),
        scratch_shapes=[pltpu.VMEM((2, TOP_K * tc * SUBLANES, LANES), jnp.float32),
                        pltpu.SemaphoreType.DMA((2,))],
        compiler_params=pltpu.CompilerParams(
            dimension_semantics=("arbitrary",), vmem_limit_bytes=VMEM_LIMIT_BYTES),
    )(d3, d3, x1, rg, fw, ys3)


def _rope_tables(positions):
    half = ROT_DIM // 2
    inv_freq = jnp.power(ROPE_THETA, -jnp.arange(half, dtype=jnp.float32) * 2.0 / ROT_DIM)
    ang = positions.astype(jnp.float32)[..., None] * inv_freq
    cs = jnp.concatenate([jnp.cos(ang), jnp.sin(ang)], axis=-1)
    j = jnp.arange(LANES) % HEAD_DIM
    f = jnp.arange(half)[:, None]
    hit = (j[None, :] < ROT_DIM) & ((j[None, :] % half) == f)
    sign = jnp.where(j < half, -1.0, 1.0)[None, :]
    zero = jnp.zeros((half, LANES), jnp.float32)
    sel = jnp.concatenate([jnp.concatenate([hit.astype(jnp.float32), zero], axis=1),
                           jnp.concatenate([zero, hit * sign], axis=1)], axis=0)
    return cs, jnp.tile(sel, (3, 1)).astype(jnp.bfloat16)


def _work_items(counts, bm, ni):
    end = jnp.cumsum(counts)
    start = end - counts
    fb = start // bm
    nblk = jnp.where(counts > 0, (end - 1) // bm - fb + 1, 0)
    iend = jnp.cumsum(nblk)
    nitems = iend[-1]
    i = jnp.minimum(jnp.arange(ni, dtype=jnp.int32), nitems - 1)
    e = jnp.sum((iend[None, :] <= i[:, None]).astype(jnp.int32), axis=1)
    e = jnp.minimum(e, N_EXPERTS - 1)
    hit = e[:, None] == jnp.arange(N_EXPERTS, dtype=jnp.int32)[None, :]
    pick = lambda table: jnp.sum(jnp.where(hit, table[None, :], 0), axis=1)
    blk = pick(fb) + i - pick(iend - nblk)
    lo = jnp.maximum(pick(start), blk * bm) - blk * bm
    hi = jnp.minimum(pick(end), (blk + 1) * bm) - blk * bm
    one = jnp.ones((1,), jnp.int32)
    first = jnp.concatenate([one, (blk[1:] != blk[:-1]).astype(jnp.int32)])
    last = jnp.concatenate([(blk[1:] != blk[:-1]).astype(jnp.int32), one])
    last = jnp.where(jnp.arange(ni) == nitems - 1, 1, last)
    newexp = jnp.concatenate([one, (e[1:] != e[:-1]).astype(jnp.int32)])
    as32 = lambda a: a.astype(jnp.int32)
    return start, (as32(blk), as32(e), as32(lo), as32(hi), first, as32(last), newexp,
                   as32(nitems).reshape(1))


def _layer(x, cs, sel, l, attn_norm_w, w_in, attn_sinks, conv_dw_w, conv_dw_b, conv_ln_w,
           conv_ln_b, attn_out_norm_w, conv_out_norm_w, w_out, ffn_norm_w, router_group_w,
           router_group_b, router_expert_w, router_expert_b, w_gate, w_up, w_down, out_norm_w):
    bsz, seq, d = x.shape
    n = bsz * seq
    assert d == SUBLANES * LANES, "token-tiled rows assume a 1024-wide model"
    bf = jnp.bfloat16
    row = lambda a: a.reshape(1, -1)
    wr = jnp.zeros((d, LANES), jnp.float32)
    wr = wr.at[:, :N_GROUPS].set(router_group_w[l]).at[:, N_GROUPS:N_GROUPS + N_EXPERTS].set(
        router_expert_w[l])
    wr_hi = wr.astype(bf)
    wr_lo = (wr - wr_hi.astype(jnp.float32)).astype(bf)
    br = jnp.zeros((1, LANES), jnp.float32)
    br = br.at[0, :N_GROUPS].set(router_group_b[l]).at[0, N_GROUPS:N_GROUPS + N_EXPERTS].set(
        router_expert_b[l])
    ts = min(SEQ_TILE, seq)
    params = dict(
        sinks=attn_sinks[l], sel=sel, anw=row(attn_norm_w[l]), win=w_in[l].astype(bf),
        dww=conv_dw_w[l], dwb=row(conv_dw_b[l]), lnw=row(conv_ln_w[l]), lnb=row(conv_ln_b[l]),
        aonw=row(attn_out_norm_w[l]), conw=row(conv_out_norm_w[l]), wout=w_out[l].astype(bf),
        fnw=row(ffn_norm_w[l]), wr=jnp.concatenate([wr_hi, wr_lo], axis=1), br=br,
        tri=jnp.tri(ts, k=-1, dtype=bf))
    x1, hf2, ri, rg, cnt = _mix(x, cs, params)

    bm = min(EXPERT_BLOCK, n * TOP_K)
    ni = (n * TOP_K) // bm + N_EXPERTS - 1
    counts = cnt[0, :N_EXPERTS].astype(jnp.int32)
    start, items = _work_items(counts, bm, ni)
    nt = n // ts
    hit = ri[:, 0:TOP_K, :, None] == jnp.arange(N_EXPERTS, dtype=jnp.int32)
    seg = jnp.sum(jnp.where(hit, start.astype(jnp.int32), 0), axis=-1)
    d3 = (seg + ri[:, TOP_K:2 * TOP_K, :]).reshape(nt, 1, TOP_K * ts)

    xs3 = _dispatch(hf2.reshape(n, SUBLANES, LANES), d3)
    ys3 = _experts(xs3, items, w_gate[l], w_up[l], w_down[l], bm)
    out = _combine(x1.reshape(n, d), rg.reshape(n, LANES), ys3, d3, row(out_norm_w))
    return out.reshape(bsz, seq, d)


def kernel(x, positions, attn_norm_w, w_in, attn_sinks, conv_dw_w, conv_dw_b, conv_ln_w, conv_ln_b, attn_out_norm_w, conv_out_norm_w, w_out, ffn_norm_w, router_group_w, router_group_b, router_expert_w, router_expert_b, w_gate, w_up, w_down, final_norm_w):
    depth = attn_norm_w.shape[0]
    assert depth == 1, "the combine kernel applies the final norm, so exactly one layer is fused"
    cs, sel = _rope_tables(positions)
    return _layer(x, cs, sel, 0, attn_norm_w, w_in, attn_sinks, conv_dw_w, conv_dw_b,
                  conv_ln_w, conv_ln_b, attn_out_norm_w, conv_out_norm_w, w_out, ffn_norm_w,
                  router_group_w, router_group_b, router_expert_w, router_expert_b,
                  w_gate, w_up, w_down, final_norm_w)
```

```python
import jax
import jax.numpy as jnp
from jax import lax
from jax.experimental import pallas as pl
from jax.experimental.pallas import tpu as pltpu

HEAD_DIM = 64
N_Q_HEADS = 8
N_KV_HEADS = 2
ATTN_WIDTH = N_Q_HEADS * HEAD_DIM
KV_WIDTH = N_KV_HEADS * HEAD_DIM
CONV_WIDTH = 512
CONV_KERNEL = 31
WINDOW = 128
ROPE_THETA = 500000.0
ROT_DIM = HEAD_DIM // 4
N_GROUPS = 4
EXPERTS_PER_GROUP = 8
N_EXPERTS = N_GROUPS * EXPERTS_PER_GROUP
TOP_K = 2
EPS = 1e-5

LANES = 128
SUBLANES = 8
VMEM_LIMIT_BYTES = 56 * 1024 * 1024

SEQ_TILE = 512
CONV_HALO = 32
CONV_CHUNK = 64
EXPERT_BLOCK = 512
DISPATCH_SLOTS = 3
EXPERT_IN_SLOTS = 3
INDEX_BATCH = 16
DMA_QUEUES = 2
NEG_BIG = -1e30

Q_END = ATTN_WIDTH
K_END = Q_END + KV_WIDTH
V_END = K_END + KV_WIDTH
CA_END = V_END + CONV_WIDTH
IN_COLS = CA_END + CONV_WIDTH


def _rms(x, w):
    return x * lax.rsqrt(jnp.mean(x * x, axis=-1, keepdims=True) + EPS) * w


def _bdot(a, b):
    return jnp.dot(a.astype(jnp.bfloat16), b.astype(jnp.bfloat16),
                   preferred_element_type=jnp.float32)


def _bdot_t(a, b):
    return lax.dot_general(a.astype(jnp.bfloat16), b.astype(jnp.bfloat16),
                           (((1,), (1,)), ((), ())), preferred_element_type=jnp.float32)


def _load_tiled(ref, base, rows):
    d = SUBLANES * LANES
    return jnp.concatenate(
        [ref[pl.ds(base + j, rows, stride=SUBLANES), :] for j in range(d // LANES)], axis=1)


def _store_tiled(ref, val):
    for j in range(val.shape[1] // LANES):
        ref[pl.ds(j, val.shape[0], stride=SUBLANES), :] = val[:, j * LANES:(j + 1) * LANES]


def _mix_kernel(sinks_ref,
                x_ref, cs_ref, sel_ref, anw_ref, win_ref, dww_ref, dwb_ref, lnw_ref, lnb_ref,
                aonw_ref, conw_ref, wout_ref, fnw_ref, wr_ref, br_ref, tri_ref,
                x1_ref, hf_ref, ri_ref, rg_ref, cnt_ref,
                kd_ref, vd_ref, ubuf_ref, ush_ref, attn_ref, run_ref):
    b = pl.program_id(0)
    s = pl.program_id(1)
    ts = x_ref.shape[1]
    nblk = ts // WINDOW
    lane = lax.broadcasted_iota(jnp.int32, (ts, LANES), 1)

    @pl.when(s == 0)
    def _():
        kd_ref[:, 0:WINDOW, :] = jnp.zeros((4, WINDOW, LANES), jnp.bfloat16)
        vd_ref[:, 0:WINDOW, :] = jnp.zeros((4, WINDOW, LANES), jnp.bfloat16)
        ubuf_ref[0:CONV_HALO, :] = jnp.zeros((CONV_HALO, CONV_WIDTH), jnp.float32)

    @pl.when((s == 0) & (b == 0))
    def _():
        run_ref[...] = jnp.zeros_like(run_ref)

    x = x_ref[0]
    hn = _rms(x, anw_ref[...]).astype(jnp.bfloat16)
    proj_c = _bdot(hn, win_ref[:, V_END:IN_COLS])

    u = proj_c[:, 0:CONV_WIDTH] * jax.nn.sigmoid(proj_c[:, CONV_WIDTH:2 * CONV_WIDTH])
    ubuf_ref[CONV_HALO:, :] = u
    span = ush_ref.shape[1]
    for sh in range(1, SUBLANES):
        ush_ref[sh - 1] = ubuf_ref[sh:sh + span, :]
    dwb = dwb_ref[...]
    off0 = CONV_HALO - (CONV_KERNEL - 1)
    convs = []
    for c in range(ts // CONV_CHUNK):
        acc = jnp.broadcast_to(dwb, (CONV_CHUNK, CONV_WIDTH))
        for t in range(CONV_KERNEL):
            a, sh = divmod(off0 + t, SUBLANES)
            r0 = c * CONV_CHUNK + a * SUBLANES
            win = (ubuf_ref[r0:r0 + CONV_CHUNK, :] if sh == 0
                   else ush_ref[sh - 1, r0:r0 + CONV_CHUNK, :])
            acc = acc + win * dww_ref[t:t + 1, :]
        convs.append(acc)
    conv = jnp.concatenate(convs, axis=0)
    ubuf_ref[0:CONV_HALO, :] = ubuf_ref[ts:ts + CONV_HALO, :]
    mu = jnp.mean(conv, axis=-1, keepdims=True)
    xc = conv - mu
    y = xc * lax.rsqrt(jnp.mean(xc * xc, axis=-1, keepdims=True) + EPS) * lnw_ref[...] + lnb_ref[...]
    conv = y * jax.nn.sigmoid(y)
    mixed_c = _rms(conv, conw_ref[...]).astype(jnp.bfloat16)

    proj_kv = _bdot(hn, win_ref[:, Q_END:V_END])
    proj_q = _bdot(hn, win_ref[:, 0:Q_END])

    cs = cs_ref[0]
    cs_hi = cs.astype(jnp.bfloat16)
    cs_r = cs - cs_hi.astype(jnp.float32)
    cs_mid = cs_r.astype(jnp.bfloat16)
    cs_lo = (cs_r - cs_mid.astype(jnp.float32)).astype(jnp.bfloat16)
    tab = jnp.dot(jnp.concatenate([cs_hi, cs_mid, cs_lo], axis=1), sel_ref[...],
                  preferred_element_type=jnp.float32)
    rotary = (lane % HEAD_DIM) < ROT_DIM
    cosv = jnp.where(rotary, tab[:, 0:LANES], 1.0)
    sinv = tab[:, LANES:2 * LANES]
    first_half = (lane % HEAD_DIM) < (ROT_DIM // 2)

    def rope(t):
        partner = jnp.where(first_half, pltpu.roll(t, LANES - ROT_DIM // 2, 1),
                            pltpu.roll(t, ROT_DIM // 2, 1))
        return t * cosv + partner * sinv

    k = rope(proj_kv[:, 0:KV_WIDTH])
    v = proj_kv[:, KV_WIDTH:2 * KV_WIDTH]
    low = lane < HEAD_DIM
    for src, dst in ((k, kd_ref), (v, vd_ref)):
        rolled = pltpu.roll(src, HEAD_DIM, 1)
        zero = jnp.zeros_like(src)
        dst[0, WINDOW:, :] = jnp.where(low, src, zero).astype(jnp.bfloat16)
        dst[1, WINDOW:, :] = jnp.where(low, zero, rolled).astype(jnp.bfloat16)
        dst[2, WINDOW:, :] = jnp.where(low, rolled, zero).astype(jnp.bfloat16)
        dst[3, WINDOW:, :] = jnp.where(low, zero, src).astype(jnp.bfloat16)

    qi = lax.broadcasted_iota(jnp.int32, (WINDOW, 2 * WINDOW), 0)
    kj = lax.broadcasted_iota(jnp.int32, (WINDOW, 2 * WINDOW), 1)
    dist = qi + WINDOW - kj
    band = (dist >= 0) & (dist < WINDOW)
    first_lo = jnp.where(s == 0, WINDOW, 0)
    lane_q = lax.broadcasted_iota(jnp.int32, (WINDOW, LANES), 1)
    for p in range(N_Q_HEADS // 2):
        g = p // 2
        qs = rope(proj_q[:, p * LANES:(p + 1) * LANES]) * (HEAD_DIM ** -0.5)
        for j in range(nblk):
            rows = slice(j * WINDOW, (j + 1) * WINDOW)
            keys = slice(j * WINDOW, (j + 2) * WINDOW)
            valid = (band & (kj >= first_lo)) if j == 0 else band
            qb = qs[rows]
            out = None
            inv = []
            for half in range(2):
                sink = sinks_ref[2 * p + half]
                sc = _bdot_t(qb, kd_ref[2 * g + half, keys, :])
                sc = jnp.where(valid, sc, NEG_BIG)
                m = jnp.maximum(jnp.max(sc, axis=-1, keepdims=True), sink)
                e = jnp.exp(sc - m)
                inv.append(1.0 / (jnp.sum(e, axis=-1, keepdims=True) + jnp.exp(sink - m)))
                o = _bdot(e, vd_ref[2 * g + half, keys, :])
                out = o if out is None else out + o
            attn_ref[rows, p * LANES:(p + 1) * LANES] = out * jnp.where(
                lane_q < HEAD_DIM, inv[0], inv[1])
    for c in range(4):
        kd_ref[c, 0:WINDOW, :] = kd_ref[c, ts:ts + WINDOW, :]
        vd_ref[c, 0:WINDOW, :] = vd_ref[c, ts:ts + WINDOW, :]

    mixed_a = _rms(attn_ref[...], aonw_ref[...])
    x1 = x + _bdot(mixed_a, wout_ref[0:ATTN_WIDTH, :]) + _bdot(mixed_c, wout_ref[ATTN_WIDTH:, :])
    x1_ref[0] = x1
    hf = _rms(x1, fnw_ref[...])
    _store_tiled(hf_ref, hf)

    hf_hi = hf.astype(jnp.bfloat16)
    hf_lo = (hf - hf_hi.astype(jnp.float32)).astype(jnp.bfloat16)
    both = jnp.dot(hf_hi, wr_ref[...], preferred_element_type=jnp.float32)
    logits = (both[:, 0:LANES] + both[:, LANES:2 * LANES]
              + jnp.dot(hf_lo, wr_ref[:, 0:LANES], preferred_element_type=jnp.float32)
              + br_ref[...])

    lanef = lane.astype(jnp.float32)
    big = float(LANES)
    gmask = lane < N_GROUPS
    gmax = jnp.max(jnp.where(gmask, logits, -jnp.inf), axis=-1, keepdims=True)
    gidx = jnp.min(jnp.where(gmask & (logits == gmax), lanef, big), axis=-1, keepdims=True)
    gsum = jnp.sum(jnp.where(gmask, jnp.exp(logits - gmax), 0.0), axis=-1, keepdims=True)
    g_p = 1.0 / gsum
    elo = N_GROUPS + EXPERTS_PER_GROUP * gidx
    emask = (lanef >= elo) & (lanef < elo + EXPERTS_PER_GROUP)
    em1 = jnp.max(jnp.where(emask, logits, -jnp.inf), axis=-1, keepdims=True)
    i1 = jnp.min(jnp.where(emask & (logits == em1), lanef, big), axis=-1, keepdims=True)
    mask2 = emask & (lanef != i1)
    em2 = jnp.max(jnp.where(mask2, logits, -jnp.inf), axis=-1, keepdims=True)
    i2 = jnp.min(jnp.where(mask2 & (logits == em2), lanef, big), axis=-1, keepdims=True)
    esum = jnp.sum(jnp.where(emask, jnp.exp(logits - em1), 0.0), axis=-1, keepdims=True)
    p1 = 1.0 / esum
    p2 = jnp.exp(em2 - em1) / esum
    gate1 = g_p * p1 / (p1 + p2)
    gate2 = g_p * p2 / (p1 + p2)
    e1 = i1 - N_GROUPS
    e2 = i2 - N_GROUPS

    oh1 = lanef == e1
    oh2 = lanef == e2
    onehot = jnp.where(oh1 | oh2, 1.0, 0.0)
    tot = jnp.dot(tri_ref[...], onehot.astype(jnp.bfloat16),
                  preferred_element_type=jnp.float32) + run_ref[0:1, :]
    r1 = jnp.sum(jnp.where(oh1, tot, 0.0), axis=-1, keepdims=True)
    r2 = jnp.sum(jnp.where(oh2, tot, 0.0), axis=-1, keepdims=True)
    run_ref[...] = run_ref[...] + jnp.sum(onehot, axis=0, keepdims=True)
    cnt_ref[...] = run_ref[...]

    ri = jnp.where(lane == 0, e1, jnp.where(lane == 1, e2, jnp.where(lane == 2, r1, r2)))
    ri_ref[0] = ri.T[0:SUBLANES, :].astype(jnp.int32)
    rg_ref[0] = jnp.where(lane == 0, gate1, gate2)


def _mix(x, cs, p):
    bsz, seq, d = x.shape
    ts = min(SEQ_TILE, seq)
    nst = seq // ts
    grid = (bsz, nst)
    tile = lambda last: pl.BlockSpec((1, ts, last), lambda b, s, *_: (b, s, 0))
    full = lambda a: pl.BlockSpec(a.shape, lambda b, s, *_: (0,) * a.ndim)
    weights = [p["sel"], p["anw"], p["win"], p["dww"], p["dwb"], p["lnw"], p["lnb"], p["aonw"], p["conw"],
               p["wout"], p["fnw"], p["wr"], p["br"], p["tri"]]
    out_shape = [
        jax.ShapeDtypeStruct((bsz, seq, d), jnp.float32),
        jax.ShapeDtypeStruct((bsz * seq * SUBLANES, LANES), jnp.float32),
        jax.ShapeDtypeStruct((bsz * nst, SUBLANES, ts), jnp.int32),
        jax.ShapeDtypeStruct((bsz, seq, LANES), jnp.float32),
        jax.ShapeDtypeStruct((SUBLANES, LANES), jnp.float32),
    ]
    out_specs = [tile(d),
                 pl.BlockSpec((ts * SUBLANES, LANES), lambda b, s, *_: (b * nst + s, 0)),
                 pl.BlockSpec((1, SUBLANES, ts), lambda b, s, *_: (b * nst + s, 0, 0)),
                 tile(LANES),
                 pl.BlockSpec((SUBLANES, LANES), lambda b, s, *_: (0, 0))]
    grid_spec = pltpu.PrefetchScalarGridSpec(
        num_scalar_prefetch=1, grid=grid,
        in_specs=[tile(d), tile(cs.shape[2])] + [full(w) for w in weights],
        out_specs=out_specs,
        scratch_shapes=[
            pltpu.VMEM((4, ts + WINDOW, LANES), jnp.bfloat16),
            pltpu.VMEM((4, ts + WINDOW, LANES), jnp.bfloat16),
            pltpu.VMEM((ts + CONV_HALO, CONV_WIDTH), jnp.float32),
            pltpu.VMEM((SUBLANES - 1, ts + CONV_HALO - SUBLANES, CONV_WIDTH), jnp.float32),
            pltpu.VMEM((ts, ATTN_WIDTH), jnp.float32),
            pltpu.VMEM((SUBLANES, LANES), jnp.float32),
        ])
    return pl.pallas_call(
        _mix_kernel, grid_spec=grid_spec, out_shape=out_shape, name="mix",
        compiler_params=pltpu.CompilerParams(
            dimension_semantics=("arbitrary", "arbitrary"),
            vmem_limit_bytes=VMEM_LIMIT_BYTES),
    )(p["sinks"], x, cs, *weights)


def _dispatch_kernel(dst_ref, hf_hbm, xs_hbm, buf_ref, in_sem, out_sem):
    i = pl.program_id(0)
    n = pl.num_programs(0)
    td = buf_ref.shape[1]
    nslot = buf_ref.shape[0]

    def tile_in(t):
        return pltpu.make_async_copy(hf_hbm.at[pl.ds(t * td, td)], buf_ref.at[t % nslot],
                                     in_sem.at[t % nslot])

    def rows(slot, start):
        src = buf_ref.at[slot]
        for q0 in range(0, TOP_K * td, INDEX_BATCH):
            qs = range(q0, q0 + INDEX_BATCH)
            idx = [dst_ref[0, 0, q] if start else 0 for q in qs]
            for q, ix in zip(qs, idx):
                cp = pltpu.make_async_copy(src.at[q % td], xs_hbm.at[ix], out_sem.at[slot])
                cp.start(priority=q % DMA_QUEUES) if start else cp.wait()

    @pl.when(i == 0)
    def _():
        tile_in(0).start()

    @pl.when((i == 0) & (n > 1))
    def _():
        tile_in(1).start()

    tile_in(i).wait()
    rows(i % nslot, True)

    @pl.when(i > 0)
    def _():
        rows((i - 1) % nslot, False)

    @pl.when(i + 2 < n)
    def _():
        tile_in(i + 2).start()

    @pl.when(i == n - 1)
    def _():
        rows(i % nslot, False)


def _dispatch(hf3, d3):
    n = hf3.shape[0]
    nt = d3.shape[0]
    td = n // nt
    return pl.pallas_call(
        _dispatch_kernel, grid=(nt,), name="dispatch",
        in_specs=[pl.BlockSpec((1, 1, TOP_K * td), lambda i: (i, 0, 0), memory_space=pltpu.SMEM),
                  pl.BlockSpec(memory_space=pl.ANY)],
        out_specs=pl.BlockSpec(memory_space=pl.ANY),
        out_shape=jax.ShapeDtypeStruct((TOP_K * n, SUBLANES, LANES), jnp.float32),
        scratch_shapes=[pltpu.VMEM((DISPATCH_SLOTS, td, SUBLANES, LANES), jnp.float32),
                        pltpu.SemaphoreType.DMA((DISPATCH_SLOTS,)),
                        pltpu.SemaphoreType.DMA((DISPATCH_SLOTS,))],
        compiler_params=pltpu.CompilerParams(dimension_semantics=("arbitrary",)),
    )(d3, hf3)


def _expert_kernel(blk_ref, exp_ref, lo_ref, hi_ref, first_ref, last_ref, newexp_ref,
                   nitems_ref,
                   xs_hbm, wg_ref, wu_ref, wd_ref, ys_hbm,
                   xbuf_ref, yacc_ref, wgb_ref, wub_ref, wdb_ref, in_sem, out_sem):
    i = pl.program_id(0)
    nitems = nitems_ref[0]
    bm = xbuf_ref.shape[2]
    nslab = xbuf_ref.shape[1]

    nxs = xbuf_ref.shape[0]

    def x_copies(item, start):
        b = blk_ref[item]
        for j in range(nslab):
            cp = pltpu.make_async_copy(xs_hbm.at[pl.ds(b * bm, bm), j],
                                       xbuf_ref.at[item % nxs, j], in_sem.at[item % nxs])
            cp.start() if start else cp.wait()

    def y_copies(b, start):
        for j in range(nslab):
            cp = pltpu.make_async_copy(yacc_ref.at[b % 2, j],
                                       ys_hbm.at[pl.ds(b * bm, bm), j], out_sem.at[b % 2])
            cp.start() if start else cp.wait()

    @pl.when(i == 0)
    def _():
        x_copies(0, True)

    @pl.when((i == 0) & (nitems > 1))
    def _():
        x_copies(1, True)

    @pl.when(i + 2 < nitems)
    def _():
        x_copies(i + 2, True)

    @pl.when((i < nitems) & (newexp_ref[i] == 1))
    def _():
        wgb_ref[...] = wg_ref[0].astype(jnp.bfloat16)
        wub_ref[...] = wu_ref[0].astype(jnp.bfloat16)
        wdb_ref[...] = wd_ref[0].astype(jnp.bfloat16)

    @pl.when(i < nitems)
    def _():
        b = blk_ref[i]
        x_copies(i, False)
        xb = jnp.concatenate([xbuf_ref[i % nxs, j] for j in range(nslab)],
                             axis=1).astype(jnp.bfloat16)
        de = wgb_ref.shape[1]
        y = None
        for c0 in range(0, de, de // 2):
            cols = slice(c0, c0 + de // 2)
            hg = jnp.dot(xb, wgb_ref[:, cols], preferred_element_type=jnp.float32)
            hu = jnp.dot(xb, wub_ref[:, cols], preferred_element_type=jnp.float32)
            h = (hg * jax.nn.sigmoid(hg) * hu).astype(jnp.bfloat16)
            part = jnp.dot(h, wdb_ref[cols, :], preferred_element_type=jnp.float32)
            y = part if y is None else y + part
        row = lax.broadcasted_iota(jnp.int32, (bm, 1), 0)
        y = jnp.where((row >= lo_ref[i]) & (row < hi_ref[i]), y, 0.0)

        @pl.when(first_ref[i] == 1)
        def _():
            @pl.when(b >= 2)
            def _():
                y_copies(b - 2, False)

            for j in range(nslab):
                yacc_ref[b % 2, j] = y[:, j * LANES:(j + 1) * LANES]

        @pl.when(first_ref[i] == 0)
        def _():
            for j in range(nslab):
                yacc_ref[b % 2, j] += y[:, j * LANES:(j + 1) * LANES]

        @pl.when(last_ref[i] == 1)
        def _():
            y_copies(b, True)

        @pl.when(i == nitems - 1)
        def _():
            y_copies(b, False)

            @pl.when(b >= 1)
            def _():
                y_copies(b - 1, False)


def _experts(xs3, items, wg, wu, wd, bm):
    rows = xs3.shape[0]
    ni = items[0].shape[0]
    d, de = wg.shape[1], wg.shape[2]
    nslab = d // LANES
    wsp = lambda shape: pl.BlockSpec(shape, lambda i, b, e, *_: (e[i], 0, 0))
    grid_spec = pltpu.PrefetchScalarGridSpec(
        num_scalar_prefetch=len(items), grid=(ni,),
        in_specs=[pl.BlockSpec(memory_space=pl.ANY),
                  wsp((1, d, de)), wsp((1, d, de)), wsp((1, de, d))],
        out_specs=pl.BlockSpec(memory_space=pl.ANY),
        scratch_shapes=[pltpu.VMEM((EXPERT_IN_SLOTS, nslab, bm, LANES), jnp.float32),
                        pltpu.VMEM((2, nslab, bm, LANES), jnp.float32),
                        pltpu.VMEM((d, de), jnp.bfloat16), pltpu.VMEM((d, de), jnp.bfloat16),
                        pltpu.VMEM((de, d), jnp.bfloat16),
                        pltpu.SemaphoreType.DMA((EXPERT_IN_SLOTS,)),
                        pltpu.SemaphoreType.DMA((2,))])
    return pl.pallas_call(
        _expert_kernel, grid_spec=grid_spec, name="experts",
        out_shape=jax.ShapeDtypeStruct((rows, SUBLANES, LANES), jnp.float32),
        compiler_params=pltpu.CompilerParams(
            dimension_semantics=("arbitrary",), vmem_limit_bytes=VMEM_LIMIT_BYTES),
    )(*items, xs3, wg, wu, wd)


def _combine_kernel(dst_ref, dstn_ref, x1_ref, rg_ref, fw_ref, ys_hbm, o_ref, buf_ref, sem_ref):
    i = pl.program_id(0)
    n = pl.num_programs(0)
    tc = x1_ref.shape[0]
    slot = i % 2

    def gather(idx_ref, sl, start):
        for r0 in range(0, TOP_K * tc, INDEX_BATCH):
            rs = range(r0, r0 + INDEX_BATCH)
            idx = [idx_ref[0, 0, r] if start else 0 for r in rs]
            for r, ix in zip(rs, idx):
                cp = pltpu.make_async_copy(ys_hbm.at[ix],
                                           buf_ref.at[sl, pl.ds(SUBLANES * r, SUBLANES), :],
                                           sem_ref.at[sl])
                cp.start(priority=r % DMA_QUEUES) if start else cp.wait()

    @pl.when(i == 0)
    def _():
        gather(dst_ref, 0, True)

    @pl.when(i + 1 < n)
    def _():
        gather(dstn_ref, 1 - slot, True)

    gather(dst_ref, slot, False)
    g = rg_ref[...]
    cur = buf_ref.at[slot]
    y = (_load_tiled(cur, 0, tc) * g[:, 0:1]
         + _load_tiled(cur, SUBLANES * tc, tc) * g[:, 1:2])
    o_ref[...] = _rms(x1_ref[...] + y, fw_ref[...])


def _combine(x1, rg, ys3, d3, fw):
    n, d = x1.shape
    nt = d3.shape[0]
    tc = n // nt
    return pl.pallas_call(
        _combine_kernel, grid=(nt,), name="combine",
        in_specs=[
            pl.BlockSpec((1, 1, TOP_K * tc), lambda i: (i, 0, 0), memory_space=pltpu.SMEM),
            pl.BlockSpec((1, 1, TOP_K * tc), lambda i: (jnp.minimum(i + 1, nt - 1), 0, 0),
                         memory_space=pltpu.SMEM),
            pl.BlockSpec((tc, d), lambda i: (i, 0)),
            pl.BlockSpec((tc, LANES), lambda i: (i, 0)),
            pl.BlockSpec((1, d), lambda i: (0, 0)),
            pl.BlockSpec(memory_space=pl.ANY),
        ],
        out_specs=pl.BlockSpec((tc, d), lambda i: (i, 0)),
        out_shape=jax.ShapeDtypeStruct((n, d), jnp.float32),
        scratch_shapes=[pltpu.VMEM((2, TOP_K * tc * SUBLANES, LANES), jnp.float32),
                        pltpu.SemaphoreType.DMA((2,))],
        compiler_params=pltpu.CompilerParams(
            dimension_semantics=("arbitrary",), vmem_limit_bytes=VMEM_LIMIT_BYTES),
    )(d3, d3, x1, rg, fw, ys3)


def _rope_tables(positions):
    half = ROT_DIM // 2
    inv_freq = jnp.power(ROPE_THETA, -jnp.arange(half, dtype=jnp.float32) * 2.0 / ROT_DIM)
    ang = positions.astype(jnp.float32)[..., None] * inv_freq
    cs = jnp.concatenate([jnp.cos(ang), jnp.sin(ang)], axis=-1)
    j = jnp.arange(LANES) % HEAD_DIM
    f = jnp.arange(half)[:, None]
    hit = (j[None, :] < ROT_DIM) & ((j[None, :] % half) == f)
    sign = jnp.where(j < half, -1.0, 1.0)[None, :]
    zero = jnp.zeros((half, LANES), jnp.float32)
    sel = jnp.concatenate([jnp.concatenate([hit.astype(jnp.float32), zero], axis=1),
                           jnp.concatenate([zero, hit * sign], axis=1)], axis=0)
    return cs, jnp.tile(sel, (3, 1)).astype(jnp.bfloat16)


def _work_items(counts, bm, ni):
    end = jnp.cumsum(counts)
    start = end - counts
    fb = start // bm
    nblk = jnp.where(counts > 0, (end - 1) // bm - fb + 1, 0)
    iend = jnp.cumsum(nblk)
    nitems = iend[-1]
    i = jnp.minimum(jnp.arange(ni, dtype=jnp.int32), nitems - 1)
    e = jnp.sum((iend[None, :] <= i[:, None]).astype(jnp.int32), axis=1)
    e = jnp.minimum(e, N_EXPERTS - 1)
    hit = e[:, None] == jnp.arange(N_EXPERTS, dtype=jnp.int32)[None, :]
    pick = lambda table: jnp.sum(jnp.where(hit, table[None, :], 0), axis=1)
    blk = pick(fb) + i - pick(iend - nblk)
    lo = jnp.maximum(pick(start), blk * bm) - blk * bm
    hi = jnp.minimum(pick(end), (blk + 1) * bm) - blk * bm
    one = jnp.ones((1,), jnp.int32)
    first = jnp.concatenate([one, (blk[1:] != blk[:-1]).astype(jnp.int32)])
    last = jnp.concatenate([(blk[1:] != blk[:-1]).astype(jnp.int32), one])
    last = jnp.where(jnp.arange(ni) == nitems - 1, 1, last)
    newexp = jnp.concatenate([one, (e[1:] != e[:-1]).astype(jnp.int32)])
    as32 = lambda a: a.astype(jnp.int32)
    return start, (as32(blk), as32(e), as32(lo), as32(hi), first, as32(last), newexp,
                   as32(nitems).reshape(1))


def _layer(x, cs, sel, l, attn_norm_w, w_in, attn_sinks, conv_dw_w, conv_dw_b, conv_ln_w,
           conv_ln_b, attn_out_norm_w, conv_out_norm_w, w_out, ffn_norm_w, router_group_w,
           router_group_b, router_expert_w, router_expert_b, w_gate, w_up, w_down, out_norm_w):
    bsz, seq, d = x.shape
    n = bsz * seq
    assert d == SUBLANES * LANES, "token-tiled rows assume a 1024-wide model"
    bf = jnp.bfloat16
    row = lambda a: a.reshape(1, -1)
    wr = jnp.zeros((d, LANES), jnp.float32)
    wr = wr.at[:, :N_GROUPS].set(router_group_w[l]).at[:, N_GROUPS:N_GROUPS + N_EXPERTS].set(
        router_expert_w[l])
    wr_hi = wr.astype(bf)
    wr_lo = (wr - wr_hi.astype(jnp.float32)).astype(bf)
    br = jnp.zeros((1, LANES), jnp.float32)
    br = br.at[0, :N_GROUPS].set(router_group_b[l]).at[0, N_GROUPS:N_GROUPS + N_EXPERTS].set(
        router_expert_b[l])
    ts = min(SEQ_TILE, seq)
    params = dict(
        sinks=attn_sinks[l], sel=sel, anw=row(attn_norm_w[l]), win=w_in[l].astype(bf),
        dww=conv_dw_w[l], dwb=row(conv_dw_b[l]), lnw=row(conv_ln_w[l]), lnb=row(conv_ln_b[l]),
        aonw=row(attn_out_norm_w[l]), conw=row(conv_out_norm_w[l]), wout=w_out[l].astype(bf),
        fnw=row(ffn_norm_w[l]), wr=jnp.concatenate([wr_hi, wr_lo], axis=1), br=br,
        tri=jnp.tri(ts, k=-1, dtype=bf))
    x1, hf2, ri, rg, cnt = _mix(x, cs, params)

    bm = min(EXPERT_BLOCK, n * TOP_K)
    ni = (n * TOP_K) // bm + N_EXPERTS - 1
    counts = cnt[0, :N_EXPERTS].astype(jnp.int32)
    start, items = _work_items(counts, bm, ni)
    nt = n // ts
    hit = ri[:, 0:TOP_K, :, None] == jnp.arange(N_EXPERTS, dtype=jnp.int32)
    seg = jnp.sum(jnp.where(hit, start.astype(jnp.int32), 0), axis=-1)
    d3 = (seg + ri[:, TOP_K:2 * TOP_K, :]).reshape(nt, 1, TOP_K * ts)

    xs3 = _dispatch(hf2.reshape(n, SUBLANES, LANES), d3)
    ys3 = _experts(xs3, items, w_gate[l], w_up[l], w_down[l], bm)
    out = _combine(x1.reshape(n, d), rg.reshape(n, LANES), ys3, d3, row(out_norm_w))
    return out.reshape(bsz, seq, d)


def kernel(x, positions, attn_norm_w, w_in, attn_sinks, conv_dw_w, conv_dw_b, conv_ln_w, conv_ln_b, attn_out_norm_w, conv_out_norm_w, w_out, ffn_norm_w, router_group_w, router_group_b, router_expert_w, router_expert_b, w_gate, w_up, w_down, final_norm_w):
    depth = attn_norm_w.shape[0]
    assert depth == 1, "the combine kernel applies the final norm, so exactly one layer is fused"
    cs, sel = _rope_tables(positions)
    return _layer(x, cs, sel, 0, attn_norm_w, w_in, attn_sinks, conv_dw_w, conv_dw_b,
                  conv_ln_w, conv_ln_b, attn_out_norm_w, conv_out_norm_w, w_out, ffn_norm_w,
                  router_group_w, router_group_b, router_expert_w, router_expert_b,
                  w_gate, w_up, w_down, final_norm_w)
```
